```python
import jax, jax.numpy as jnp
from jax import lax
import numpy as np

D_MODEL = 1024
BATCH = 8
SEQ = 2048
DEPTH = 4

SB_HEADS = 8
SB_HEAD_DIM = 64
SB_WIDTH = SB_HEADS * SB_HEAD_DIM
SB_BLOCK = 128

GDN_HEADS = 4
GDN_HEAD_DIM = 128
GDN_WIDTH = GDN_HEADS * GDN_HEAD_DIM
GDN_CHUNK = 64
CONV_K = 4

EPS = 1e-6

SPLIT_SIZES = (3 * SB_WIDTH, SB_WIDTH, 3 * GDN_WIDTH, GDN_WIDTH, GDN_HEADS, GDN_HEADS, D_MODEL, D_MODEL)
N_IN = 3 * SB_WIDTH + SB_WIDTH + 3 * GDN_WIDTH + GDN_WIDTH + 2 * GDN_HEADS + 2 * D_MODEL

kernel_name = "stickbreak_gdn_gated_hybrid"


def rms_norm(x, w):
    xf = x.astype(jnp.float32)
    y = xf * lax.rsqrt(jnp.mean(xf * xf, axis=-1, keepdims=True) + EPS)
    return (y * w.astype(jnp.float32)).astype(x.dtype)


def l2_normalize(x):
    return x * lax.rsqrt(jnp.sum(x * x, axis=-1, keepdims=True) + EPS)


def split_heads(t, n_heads, head_dim):
    b, s, _ = t.shape
    return t.reshape(b, s, n_heads, head_dim).transpose(0, 2, 1, 3)


def merge_heads(t):
    b, h, s, d = t.shape
    return t.transpose(0, 2, 1, 3).reshape(b, s, h * d)


def stick_breaking_attention(q, k, v):
    seq = q.shape[2]
    scale = SB_HEAD_DIM ** -0.5
    outs = []
    for blk in range(seq // SB_BLOCK):
        t0 = blk * SB_BLOCK
        t1 = t0 + SB_BLOCK
        qb = q[:, :, t0:t1].astype(jnp.float32)
        kb = k[:, :, :t1].astype(jnp.float32)
        vb = v[:, :, :t1].astype(jnp.float32)
        z = jnp.einsum('bhqd,bhkd->bhqk', qb, kb) * scale
        q_pos = t0 + jnp.arange(SB_BLOCK)[:, None]
        k_pos = jnp.arange(t1)[None, :]
        causal = k_pos < q_pos
        log_keep = jnp.where(causal, -jax.nn.softplus(z), 0.0)
        suffix = lax.cumsum(log_keep, axis=3, reverse=True) - log_keep
        log_a = jax.nn.log_sigmoid(z) + suffix
        a = jnp.where(causal, jnp.exp(log_a), 0.0)
        outs.append(jnp.einsum('bhqk,bhkd->bhqd', a, vb))
    return jnp.concatenate(outs, axis=2).astype(q.dtype)


def causal_depthwise_conv(x, w):
    channels = x.shape[-1]
    return lax.conv_general_dilated(
        x, w[:, None, :].astype(x.dtype), window_strides=(1,), padding=[(CONV_K - 1, 0)],
        dimension_numbers=('NWC', 'WIO', 'NWC'), feature_group_count=channels)


def gated_delta_rule_chunked(q, k, v, g, beta):
    b, h, seq, dk = q.shape
    dv = v.shape[-1]
    c = GDN_CHUNK
    n = seq // c
    q = q * dk ** -0.5
    q = q.reshape(b, h, n, c, dk)
    k = k.reshape(b, h, n, c, dk)
    v = v.reshape(b, h, n, c, dv)
    beta = beta.reshape(b, h, n, c)
    g = jnp.cumsum(g.reshape(b, h, n, c), axis=-1)
    idx = jnp.arange(c)
    lower_incl = idx[:, None] >= idx[None, :]
    strict = idx[:, None] > idx[None, :]
    decay = jnp.exp(jnp.where(lower_incl, g[..., :, None] - g[..., None, :], -jnp.inf))
    kk = jnp.einsum('bhnid,bhnjd->bhnij', k, k)
    m = jnp.where(strict, beta[..., :, None] * kk * decay, 0.0)
    rhs = jnp.concatenate([v * beta[..., None], k * (beta * jnp.exp(g))[..., None]], axis=-1)
    sol = lax.linalg.triangular_solve(m, rhs, left_side=True, lower=True, unit_diagonal=True)
    u = sol[..., :dv]
    w = sol[..., dv:]
    intra = jnp.where(lower_incl, jnp.einsum('bhnid,bhnjd->bhnij', q, k) * decay, 0.0)

    def step(state, inp):
        q_c, k_c, u_c, w_c, g_c, intra_c = inp
        v_new = u_c - jnp.einsum('bhcd,bhde->bhce', w_c, state)
        o = (jnp.einsum('bhcd,bhde->bhce', q_c * jnp.exp(g_c)[..., None], state)
             + jnp.einsum('bhij,bhje->bhie', intra_c, v_new))
        g_last = g_c[..., -1:]
        state = (state * jnp.exp(g_last)[..., None]
                 + jnp.einsum('bhcd,bhce->bhde', k_c * jnp.exp(g_last - g_c)[..., None], v_new))
        return state, o

    chunk_first = lambda t: jnp.moveaxis(t, 2, 0)
    xs = tuple(chunk_first(t) for t in (q, k, u, w, g, intra))
    state0 = jnp.zeros((b, h, dk, dv), jnp.float32)
    _, o = lax.scan(step, state0, xs)
    return jnp.moveaxis(o, 0, 2).reshape(b, h, seq, dv)


def hybrid_layer(x, pre_w, post_w, w_in, conv_w, a_log, dt_bias, gdn_norm_w,
                 w_branch_sb, w_branch_gdn, w_out):
    f32 = jnp.float32
    h = rms_norm(x, pre_w)
    proj = h @ w_in
    points = [int(p) for p in np.cumsum(SPLIT_SIZES)[:-1]]
    sb_qkv, sb_gate, gdn_qkv, gdn_gate, gdn_a, gdn_b, merge_sb, merge_gdn = jnp.split(proj, points, axis=-1)

    q_sb, k_sb, v_sb = [split_heads(t, SB_HEADS, SB_HEAD_DIM) for t in jnp.split(sb_qkv, 3, axis=-1)]
    o_sb = merge_heads(stick_breaking_attention(q_sb, k_sb, v_sb)) * jax.nn.silu(sb_gate)

    gdn_qkv = jax.nn.silu(causal_depthwise_conv(gdn_qkv, conv_w))
    q_g, k_g, v_g = [split_heads(t, GDN_HEADS, GDN_HEAD_DIM).astype(f32) for t in jnp.split(gdn_qkv, 3, axis=-1)]
    q_g = l2_normalize(q_g)
    k_g = l2_normalize(k_g)
    g = (-jnp.exp(a_log.astype(f32)) * jax.nn.softplus(gdn_a.astype(f32) + dt_bias.astype(f32))).transpose(0, 2, 1)
    beta = jax.nn.sigmoid(gdn_b.astype(f32)).transpose(0, 2, 1)
    o_g = gated_delta_rule_chunked(q_g, k_g, v_g, g, beta)
    o_g = merge_heads(rms_norm(o_g, gdn_norm_w)).astype(x.dtype) * jax.nn.silu(gdn_gate)

    merged = (jax.nn.sigmoid(merge_sb) * (o_sb @ w_branch_sb)
              + jax.nn.sigmoid(merge_gdn) * (o_g @ w_branch_gdn))
    y = merged @ w_out
    return x + rms_norm(y, post_w)


def setup_inputs(seed: int = 0) -> dict:
    key = jax.random.key(seed)
    ks = jax.random.split(key, 12)
    f32 = jnp.float32

    def normal(k, shape, scale):
        return jax.random.normal(k, shape, f32) * scale

    x = normal(ks[0], (BATCH, SEQ, D_MODEL), 1.0)
    pre_norm_w = 1.0 + normal(ks[1], (DEPTH, D_MODEL), 0.05)
    post_norm_w = 1.0 + normal(ks[2], (DEPTH, D_MODEL), 0.05)
    w_in = normal(ks[3], (DEPTH, D_MODEL, N_IN), D_MODEL ** -0.5)
    conv_w = normal(ks[4], (DEPTH, CONV_K, 3 * GDN_WIDTH), CONV_K ** -0.5)
    a_log = jnp.log(jax.random.uniform(ks[5], (DEPTH, GDN_HEADS), f32, minval=0.5, maxval=16.0))
    dt = jnp.exp(jax.random.uniform(ks[6], (DEPTH, GDN_HEADS), f32,
                                    minval=math_log(0.001), maxval=math_log(0.1)))
    dt_bias = dt + jnp.log(-jnp.expm1(-dt))
    gdn_norm_w = 1.0 + normal(ks[7], (DEPTH, GDN_HEAD_DIM), 0.05)
    w_branch_sb = normal(ks[8], (DEPTH, SB_WIDTH, D_MODEL), SB_WIDTH ** -0.5)
    w_branch_gdn = normal(ks[9], (DEPTH, GDN_WIDTH, D_MODEL), GDN_WIDTH ** -0.5)
    w_out = normal(ks[10], (DEPTH, D_MODEL, D_MODEL), D_MODEL ** -0.5)
    return {"x": x, "pre_norm_w": pre_norm_w, "post_norm_w": post_norm_w, "w_in": w_in,
            "conv_w": conv_w, "a_log": a_log, "dt_bias": dt_bias, "gdn_norm_w": gdn_norm_w,
            "w_branch_sb": w_branch_sb, "w_branch_gdn": w_branch_gdn, "w_out": w_out}


def math_log(v):
    return float(np.log(v))


def reference(x, pre_norm_w, post_norm_w, w_in, conv_w, a_log, dt_bias, gdn_norm_w,
              w_branch_sb, w_branch_gdn, w_out):
    for layer in range(DEPTH):
        x = hybrid_layer(x, pre_norm_w[layer], post_norm_w[layer], w_in[layer], conv_w[layer],
                         a_log[layer], dt_bias[layer], gdn_norm_w[layer],
                         w_branch_sb[layer], w_branch_gdn[layer], w_out[layer])
    return x
```

```python
import functools

import jax
import jax.numpy as jnp
from jax import lax
from jax.experimental import pallas as pl
from jax.experimental.pallas import tpu as pltpu

F32 = jnp.float32
BF16 = jnp.bfloat16

D_MODEL = 1024
SB_HEADS = 8
SB_HEAD_DIM = 64
SB_WIDTH = SB_HEADS * SB_HEAD_DIM
SB_BLOCK = 128
GDN_HEADS = 4
GDN_HEAD_DIM = 128
GDN_WIDTH = GDN_HEADS * GDN_HEAD_DIM
GDN_CHUNK = 64
CONV_K = 4
EPS = 1e-6

LANES = 128
GDN_GROUP = 256
CHUNKS_PER_GROUP = GDN_GROUP // GDN_CHUNK

REST_MERGE_SB = 0
REST_MERGE_GDN = 8
REST_SB_GATE = 16
REST_GDN_Q = 20
REST_GDN_K = 24
REST_GDN_V = 28
REST_GDN_GATE = 32
REST_AB = 36
REST_WIDTH = (REST_AB + GDN_HEADS) * LANES
QKV_WIDTH = 3 * SB_WIDTH

VMEM_LIMIT = 56 * 1024 * 1024


def _split_bf16(x):
    hi = x.astype(BF16)
    lo = (x - hi.astype(F32)).astype(BF16)
    return hi, lo


def _dot(a, b, dims=None):
    if dims is None:
        return jnp.dot(a, b, preferred_element_type=F32)
    return lax.dot_general(a, b, (dims, ((), ())), preferred_element_type=F32)


def _dot3(a, b, dims=None):
    a_hi, a_lo = _split_bf16(a)
    b_hi, b_lo = _split_bf16(b)
    return _dot(a_hi, b_hi, dims) + (_dot(a_hi, b_lo, dims) + _dot(a_lo, b_hi, dims))


def _sigmoid(x):
    return 1.0 / (1.0 + jnp.exp(-x))


def _silu(x):
    return x * _sigmoid(x)


def _softplus(x):
    return jnp.maximum(x, 0.0) + jnp.log(1.0 + jnp.exp(-jnp.abs(x)))


def _in_proj_kernel(x_ref, pw_ref, wa_ref, wb_ref, qkv_ref, rest_ref, *, col_chunk):
    x = x_ref[...]
    ms = jnp.mean(x * x, axis=-1, keepdims=True)
    h = (x * lax.rsqrt(ms + EPS) * pw_ref[...]).astype(BF16)
    for c in range(QKV_WIDTH // col_chunk):
        sl = slice(c * col_chunk, (c + 1) * col_chunk)
        qkv_ref[:, sl] = _dot(h, wa_ref[:, sl]).astype(BF16)
    for c in range(REST_WIDTH // col_chunk):
        sl = slice(c * col_chunk, (c + 1) * col_chunk)
        rest_ref[:, sl] = _dot(h, wb_ref[:, sl])


def _in_proj(x2d, pre_w, wa, wb, *, tm=256):
    n = x2d.shape[0]
    const = lambda i: (0, 0)
    return pl.pallas_call(
        functools.partial(_in_proj_kernel, col_chunk=512),
        out_shape=(jax.ShapeDtypeStruct((n, QKV_WIDTH), BF16),
                   jax.ShapeDtypeStruct((n, REST_WIDTH), F32)),
        grid=(n // tm,),
        in_specs=[
            pl.BlockSpec((tm, D_MODEL), lambda i: (i, 0)),
            pl.BlockSpec((1, D_MODEL), const),
            pl.BlockSpec((D_MODEL, QKV_WIDTH), const, pipeline_mode=pl.Buffered(1)),
            pl.BlockSpec((D_MODEL, REST_WIDTH), const, pipeline_mode=pl.Buffered(1)),
        ],
        out_specs=(pl.BlockSpec((tm, QKV_WIDTH), lambda i: (i, 0)),
                   pl.BlockSpec((tm, REST_WIDTH), lambda i: (i, 0))),
        compiler_params=pltpu.CompilerParams(
            dimension_semantics=("arbitrary",), vmem_limit_bytes=VMEM_LIMIT),
        name="in_proj",
    )(x2d, pre_w, wa, wb)


def _sb_kernel(q_ref, k_ref, v_ref, gate_ref, o_ref):
    i = pl.program_id(2)
    blk = SB_BLOCK
    lane = lax.broadcasted_iota(jnp.int32, (blk, LANES), 1)
    row = lax.broadcasted_iota(jnp.int32, (blk, blk), 0)
    col = lax.broadcasted_iota(jnp.int32, (blk, blk), 1)
    first = lane < SB_HEAD_DIM
    causal = col < row
    suffix_rhs = jnp.concatenate(
        [(row > col).astype(BF16), jnp.ones((blk, blk), BF16)], axis=1)

    q = q_ref[0] * jnp.asarray(SB_HEAD_DIM ** -0.5, BF16)
    zero = jnp.zeros_like(q)
    q_heads = (jnp.where(first, q, zero), jnp.where(first, zero, q))

    def tile(qh, kb, vb, carry, acc, diag):
        z = _dot(qh, kb, ((1,), (1,)))
        sp = _softplus(z)
        log_keep = -sp
        if diag:
            log_keep = jnp.where(causal, log_keep, 0.0)
        hi, lo = _split_bf16(log_keep)
        sums = _dot(hi, suffix_rhs) + _dot(lo, suffix_rhs)
        log_a = (z - sp) + (carry + sums[:, :blk])
        a = jnp.exp(log_a)
        if diag:
            a = jnp.where(causal, a, 0.0)
        acc = acc + _dot(a.astype(BF16), vb)
        return carry + sums[:, blk:], acc

    def load_kv(j):
        start = pl.multiple_of(j * blk, blk)
        return k_ref[0, pl.ds(start, blk), :], v_ref[0, pl.ds(start, blk), :]

    zeros = jnp.zeros((blk, blk), F32)
    kb, vb = load_kv(i)
    state = []
    for qh in q_heads:
        state.extend(tile(qh, kb, vb, zeros, zeros, True))

    def body(jj, st):
        kb, vb = load_kv(i - 1 - jj)
        c0, a0 = tile(q_heads[0], kb, vb, st[0], st[1], False)
        c1, a1 = tile(q_heads[1], kb, vb, st[2], st[3], False)
        return (c0, a0, c1, a1)

    st = lax.fori_loop(0, i, body, tuple(state))
    o = jnp.where(first, st[1], st[3])
    o_ref[0] = (o * _silu(gate_ref[0])).astype(BF16)


def _sb_attention(qkv, rest, batch, seq):
    pairs = SB_WIDTH // LANES
    qkv3 = qkv.reshape(batch, seq, QKV_WIDTH)
    rest3 = rest.reshape(batch, seq, REST_WIDTH)
    return pl.pallas_call(
        _sb_kernel,
        out_shape=jax.ShapeDtypeStruct((batch, seq, SB_WIDTH), BF16),
        grid=(batch, pairs, seq // SB_BLOCK),
        in_specs=[
            pl.BlockSpec((1, SB_BLOCK, LANES), lambda b, p, i: (b, i, p)),
            pl.BlockSpec((1, seq, LANES), lambda b, p, i: (b, 0, pairs + p)),
            pl.BlockSpec((1, seq, LANES), lambda b, p, i: (b, 0, 2 * pairs + p)),
            pl.BlockSpec((1, SB_BLOCK, LANES), lambda b, p, i: (b, i, REST_SB_GATE + p)),
        ],
        out_specs=pl.BlockSpec((1, SB_BLOCK, LANES), lambda b, p, i: (b, i, p)),
        compiler_params=pltpu.CompilerParams(
            dimension_semantics=("arbitrary", "arbitrary", "arbitrary"),
            vmem_limit_bytes=VMEM_LIMIT),
        name="sb_attn",
    )(qkv3, qkv3, qkv3, rest3)


def _shift_rows(x, s, row):
    return jnp.where(row >= s, pltpu.roll(x, s, 0), 0.0)


def _gdn_kernel(q_ref, k_ref, v_ref, gate_ref, ab_ref, cwq_ref, cwk_ref, cwv_ref,
                alog_ref, dtb_ref, nw_ref, o_ref,
                qn_s, kn_s, vc_s, g_s, gl_s, beta_s, n_s, b_s, qh_s, gam_s, o_s):
    seq = q_ref.shape[1]
    n_groups = seq // GDN_GROUP
    n_chunks = seq // GDN_CHUNK
    dk = GDN_HEAD_DIM
    row = lax.broadcasted_iota(jnp.int32, (seq, LANES), 0)

    def conv_silu(x_ref, w_ref):
        x = x_ref[0]
        w = w_ref[...]
        y = x * w[CONV_K - 1:CONV_K, :]
        for s in range(1, CONV_K):
            y = y + _shift_rows(x, s, row) * w[CONV_K - 1 - s:CONV_K - s, :]
        return _silu(y)

    def l2norm(x):
        return x * lax.rsqrt(jnp.sum(x * x, axis=-1, keepdims=True) + EPS)

    qn_s[...] = l2norm(conv_silu(q_ref, cwq_ref)) * (dk ** -0.5)
    kn_s[...] = l2norm(conv_silu(k_ref, cwk_ref))
    vc_s[...] = conv_silu(v_ref, cwv_ref)

    ab = ab_ref[0]
    g_raw = -jnp.exp(alog_ref[0]) * _softplus(ab + dtb_ref[0])
    in_chunk = row % GDN_CHUNK
    g = g_raw
    s = 1
    while s < GDN_CHUNK:
        g = g + jnp.where(in_chunk >= s, pltpu.roll(g, s, 0), 0.0)
        s *= 2
    g_col = jnp.broadcast_to(g[:, 0:1], (seq, LANES))
    g_s[...] = g_col
    g3 = g_col.reshape(n_chunks, GDN_CHUNK, LANES)
    gl_s[...] = jnp.broadcast_to(g3[:, GDN_CHUNK - 1:GDN_CHUNK, :], g3.shape).reshape(seq, LANES)
    beta_s[...] = jnp.broadcast_to(_sigmoid(ab[:, 1:2]), (seq, LANES))

    gi = lax.broadcasted_iota(jnp.int32, (GDN_GROUP, GDN_GROUP), 0)
    gj = lax.broadcasted_iota(jnp.int32, (GDN_GROUP, GDN_GROUP), 1)
    same = (gi // GDN_CHUNK) == (gj // GDN_CHUNK)
    lower = same & (gi >= gj)
    strict = same & (gi > gj)
    eye = (gi == gj).astype(F32)
    lane_chunk = lax.broadcasted_iota(jnp.int32, (dk, GDN_GROUP), 1) // GDN_CHUNK

    def group(r, _):
        r0 = pl.multiple_of(r * GDN_GROUP, GDN_GROUP)
        rows = pl.ds(r0, GDN_GROUP)
        kg = kn_s[rows, :]
        qg = qn_s[rows, :]
        vg = vc_s[rows, :]
        gc = g_s[rows, :]
        gl = gl_s[rows, :]
        bc = beta_s[rows, :]
        g_rowvec = gc.T[0:1, :]
        gcc = jnp.concatenate([gc, gc], axis=1)
        bcc = jnp.concatenate([bc, bc], axis=1)
        dlog = gcc - g_rowvec
        decay = jnp.where(lower, jnp.exp(jnp.where(lower, dlog, 0.0)), 0.0)
        kk = _dot3(kg, kg, ((1,), (1,)))
        x = jnp.where(strict, -(bcc * kk * decay), 0.0)
        p = eye + x
        y = x
        for _ in range(5):
            y = _dot3(y, y)
            p = p + _dot3(p, y)
        rhs = jnp.concatenate([vg * bc, kg * (bc * jnp.exp(gc))], axis=1)
        uw = _dot3(p, rhs)
        qk = _dot(qg.astype(BF16), kg.astype(BF16), ((1,), (1,)))
        intra = jnp.where(lower, qk * decay, 0.0)
        iuw = _dot3(intra, uw)
        o_s[rows, :] = iuw[:, :dk]
        qh_s[rows, :] = (qg * jnp.exp(gc) - iuw[:, dk:]).astype(BF16)
        k_til_t = (kg * jnp.exp(gl - gc)).T
        for c in range(CHUNKS_PER_GROUP):
            bn = _dot3(jnp.where(lane_chunk == c, k_til_t, 0.0), uw)
            idx = r * CHUNKS_PER_GROUP + c
            b_s[idx] = bn[:, :dk]
            n_s[idx] = (-bn[:, dk:]).astype(BF16)
            gam_s[idx] = jnp.exp(gl[c * GDN_CHUNK:c * GDN_CHUNK + 8, :])
        return 0

    lax.fori_loop(0, n_groups, group, 0)

    def scan(c, state):
        rows = pl.ds(pl.multiple_of(c * GDN_CHUNK, GDN_CHUNK), GDN_CHUNK)
        sb = state.astype(BF16)
        o_s[rows, :] = o_s[rows, :] + _dot(qh_s[rows, :], sb)
        gam = gam_s[c][0:1, :]
        return state * gam + (_dot(n_s[c], sb) + b_s[c])

    lax.fori_loop(0, n_chunks, scan, jnp.zeros((dk, dk), F32))

    o = o_s[...]
    ms = jnp.mean(o * o, axis=-1, keepdims=True)
    o = o * lax.rsqrt(ms + EPS) * nw_ref[...]
    o_ref[0] = (o * _silu(gate_ref[0])).astype(BF16)


def _gdn(rest, conv_w, alog_b, dtb_b, norm_w, batch, seq):
    rest3 = rest.reshape(batch, seq, REST_WIDTH)
    n_chunks = seq // GDN_CHUNK
    dk = GDN_HEAD_DIM

    def seq_spec(base):
        return pl.BlockSpec((1, seq, LANES), lambda b, h: (b, 0, base + h))

    def conv_spec(base):
        return pl.BlockSpec((CONV_K, LANES), lambda b, h: (0, base + h))

    head_vec = pl.BlockSpec((1, 1, LANES), lambda b, h: (h, 0, 0))
    return pl.pallas_call(
        _gdn_kernel,
        out_shape=jax.ShapeDtypeStruct((batch, seq, GDN_WIDTH), BF16),
        grid=(batch, GDN_HEADS),
        in_specs=[
            seq_spec(REST_GDN_Q), seq_spec(REST_GDN_K), seq_spec(REST_GDN_V),
            seq_spec(REST_GDN_GATE), seq_spec(REST_AB),
            conv_spec(0), conv_spec(GDN_HEADS), conv_spec(2 * GDN_HEADS),
            head_vec, head_vec,
            pl.BlockSpec((1, LANES), lambda b, h: (0, 0)),
        ],
        out_specs=pl.BlockSpec((1, seq, LANES), lambda b, h: (b, 0, h)),
        scratch_shapes=[
            pltpu.VMEM((seq, dk), F32),
            pltpu.VMEM((seq, dk), F32),
            pltpu.VMEM((seq, dk), F32),
            pltpu.VMEM((seq, LANES), F32),
            pltpu.VMEM((seq, LANES), F32),
            pltpu.VMEM((seq, LANES), F32),
            pltpu.VMEM((n_chunks, dk, dk), BF16),
            pltpu.VMEM((n_chunks, dk, dk), F32),
            pltpu.VMEM((seq, dk), BF16),
            pltpu.VMEM((n_chunks, 8, LANES), F32),
            pltpu.VMEM((seq, dk), F32),
        ],
        compiler_params=pltpu.CompilerParams(
            dimension_semantics=("arbitrary", "arbitrary"), vmem_limit_bytes=VMEM_LIMIT),
        name="gdn",
    )(rest3, rest3, rest3, rest3, rest3, conv_w, conv_w, conv_w, alog_b, dtb_b, norm_w)


def _out_proj_kernel(x_ref, osb_ref, og_ref, msb_ref, mg_ref, wsb_ref, wg_ref, wo_ref, pw_ref, o_ref):
    ya = _dot(osb_ref[...], wsb_ref[...])
    yb = _dot(og_ref[...], wg_ref[...])
    merged = _sigmoid(msb_ref[...]) * ya + _sigmoid(mg_ref[...]) * yb
    y = _dot(merged.astype(BF16), wo_ref[...])
    ms = jnp.mean(y * y, axis=-1, keepdims=True)
    o_ref[...] = x_ref[...] + y * lax.rsqrt(ms + EPS) * pw_ref[...]


def _out_proj(x2d, o_sb, o_g, rest, w_sb, w_g, w_o, post_w, *, tm=256):
    n = x2d.shape[0]
    const = lambda i: (0, 0)
    merge_blocks = D_MODEL // LANES
    return pl.pallas_call(
        _out_proj_kernel,
        out_shape=jax.ShapeDtypeStruct((n, D_MODEL), F32),
        grid=(n // tm,),
        in_specs=[
            pl.BlockSpec((tm, D_MODEL), lambda i: (i, 0)),
            pl.BlockSpec((tm, SB_WIDTH), lambda i: (i, 0)),
            pl.BlockSpec((tm, GDN_WIDTH), lambda i: (i, 0)),
            pl.BlockSpec((tm, D_MODEL), lambda i: (i, REST_MERGE_SB // merge_blocks)),
            pl.BlockSpec((tm, D_MODEL), lambda i: (i, REST_MERGE_GDN // merge_blocks)),
            pl.BlockSpec((SB_WIDTH, D_MODEL), const),
            pl.BlockSpec((GDN_WIDTH, D_MODEL), const),
            pl.BlockSpec((D_MODEL, D_MODEL), const),
            pl.BlockSpec((1, D_MODEL), const),
        ],
        out_specs=pl.BlockSpec((tm, D_MODEL), lambda i: (i, 0)),
        compiler_params=pltpu.CompilerParams(
            dimension_semantics=("arbitrary",), vmem_limit_bytes=VMEM_LIMIT),
        name="out_proj",
    )(x2d, o_sb, o_g, rest, rest, w_sb, w_g, w_o, post_w)


def _regroup_w_in(w):
    o_sb_gate = QKV_WIDTH
    o_gdn_qkv = o_sb_gate + SB_WIDTH
    o_gdn_gate = o_gdn_qkv + 3 * GDN_WIDTH
    o_a = o_gdn_gate + GDN_WIDTH
    o_b = o_a + GDN_HEADS
    o_merge = o_b + GDN_HEADS
    wa = w[:, :QKV_WIDTH]
    ab_blocks = []
    for h in range(GDN_HEADS):
        ab_blocks.append(w[:, o_a + h:o_a + h + 1])
        ab_blocks.append(w[:, o_b + h:o_b + h + 1])
        ab_blocks.append(jnp.zeros((w.shape[0], LANES - 2), w.dtype))
    wb = jnp.concatenate([w[:, o_merge:], w[:, o_sb_gate:o_a]] + ab_blocks, axis=1)
    return wa.astype(BF16), wb.astype(BF16)


def kernel(x, pre_norm_w, post_norm_w, w_in, conv_w, a_log, dt_bias, gdn_norm_w,
           w_branch_sb, w_branch_gdn, w_out):
    batch, seq, d = x.shape
    depth = w_in.shape[0]
    assert d == D_MODEL and seq % GDN_GROUP == 0 and (batch * seq) % 256 == 0
    x2d = x.reshape(batch * seq, d)
    for layer in range(depth):
        wa, wb = _regroup_w_in(w_in[layer])
        alog_b = jnp.broadcast_to(a_log[layer].astype(F32)[:, None, None], (GDN_HEADS, 1, LANES))
        dtb_b = jnp.broadcast_to(dt_bias[layer].astype(F32)[:, None, None], (GDN_HEADS, 1, LANES))
        qkv, rest = _in_proj(x2d, pre_norm_w[layer][None, :], wa, wb)
        o_sb = _sb_attention(qkv, rest, batch, seq)
        o_g = _gdn(rest, conv_w[layer], alog_b, dtb_b, gdn_norm_w[layer][None, :], batch, seq)
        x2d = _out_proj(x2d, o_sb.reshape(batch * seq, SB_WIDTH), o_g.reshape(batch * seq, GDN_WIDTH),
                        rest, w_branch_sb[layer].astype(BF16), w_branch_gdn[layer].astype(BF16),
                        w_out[layer].astype(BF16), post_norm_w[layer][None, :])
    return x2d.reshape(batch, seq, d)
```

```python
import functools

import jax
import jax.numpy as jnp
from jax import lax
from jax.experimental import pallas as pl
from jax.experimental.pallas import tpu as pltpu

F32 = jnp.float32
BF16 = jnp.bfloat16

D_MODEL = 1024
SB_HEADS = 8
SB_HEAD_DIM = 64
SB_WIDTH = SB_HEADS * SB_HEAD_DIM
SB_TILE = 256
GDN_HEADS = 4
GDN_HEAD_DIM = 128
GDN_WIDTH = GDN_HEADS * GDN_HEAD_DIM
GDN_CHUNK = 64
CONV_K = 4
EPS = 1e-6

LANES = 128
GDN_GROUP = 256
CHUNKS_PER_GROUP = GDN_GROUP // GDN_CHUNK

REST_MERGE_SB = 0
REST_MERGE_GDN = 8
REST_SB_GATE = 16
REST_GDN_Q = 20
REST_GDN_K = 24
REST_GDN_V = 28
REST_GDN_GATE = 32
REST_AB = 36
REST_WIDTH = (REST_AB + GDN_HEADS) * LANES
QKV_WIDTH = 3 * SB_WIDTH

VMEM_LIMIT = 56 * 1024 * 1024

LOG2_E = 1.4426950408889634
MASKED_LOG = -1e30


def _split_bf16(x):
    hi = x.astype(BF16)
    lo = (x - hi.astype(F32)).astype(BF16)
    return hi, lo


def _dot(a, b, dims=None):
    if dims is None:
        return jnp.dot(a, b, preferred_element_type=F32)
    return lax.dot_general(a, b, (dims, ((), ())), preferred_element_type=F32)


def _dot3(a, b, dims=None):
    a_hi, a_lo = _split_bf16(a)
    b_hi, b_lo = _split_bf16(b)
    return _dot(a_hi, b_hi, dims) + (_dot(a_hi, b_lo, dims) + _dot(a_lo, b_hi, dims))


def _sigmoid(x):
    return 1.0 / (1.0 + jnp.exp(-x))


def _silu(x):
    return x * _sigmoid(x)


def _softplus(x):
    return jnp.maximum(x, 0.0) + jnp.log(1.0 + jnp.exp(-jnp.abs(x)))


def _in_proj_kernel(x_ref, pw_ref, wa_ref, wb_ref, qkv_ref, rest_ref, *, col_chunk):
    x = x_ref[...]
    ms = jnp.mean(x * x, axis=-1, keepdims=True)
    h = (x * lax.rsqrt(ms + EPS) * pw_ref[...]).astype(BF16)
    for c in range(QKV_WIDTH // col_chunk):
        sl = slice(c * col_chunk, (c + 1) * col_chunk)
        qkv_ref[:, sl] = _dot(h, wa_ref[:, sl]).astype(BF16)
    for c in range(REST_WIDTH // col_chunk):
        sl = slice(c * col_chunk, (c + 1) * col_chunk)
        rest_ref[:, sl] = _dot(h, wb_ref[:, sl])


def _in_proj(x2d, pre_w, wa, wb, *, tm=256):
    n = x2d.shape[0]
    const = lambda i: (0, 0)
    return pl.pallas_call(
        functools.partial(_in_proj_kernel, col_chunk=512),
        out_shape=(jax.ShapeDtypeStruct((n, QKV_WIDTH), BF16),
                   jax.ShapeDtypeStruct((n, REST_WIDTH), F32)),
        grid=(n // tm,),
        in_specs=[
            pl.BlockSpec((tm, D_MODEL), lambda i: (i, 0)),
            pl.BlockSpec((1, D_MODEL), const),
            pl.BlockSpec((D_MODEL, QKV_WIDTH), const, pipeline_mode=pl.Buffered(1)),
            pl.BlockSpec((D_MODEL, REST_WIDTH), const, pipeline_mode=pl.Buffered(1)),
        ],
        out_specs=(pl.BlockSpec((tm, QKV_WIDTH), lambda i: (i, 0)),
                   pl.BlockSpec((tm, REST_WIDTH), lambda i: (i, 0))),
        compiler_params=pltpu.CompilerParams(
            dimension_semantics=("arbitrary",), vmem_limit_bytes=VMEM_LIMIT),
        name="in_proj",
    )(x2d, pre_w, wa, wb)


def _sb_kernel(q_ref, k_ref, v_ref, gate_ref, o_ref, e_s, acc_s):
    i = pl.program_id(2)
    tile = SB_TILE
    lane = lax.broadcasted_iota(jnp.int32, (tile, LANES), 1)
    row = lax.broadcasted_iota(jnp.int32, (tile, tile), 0)
    col = lax.broadcasted_iota(jnp.int32, (tile, tile), 1)
    first = lane < SB_HEAD_DIM
    causal = col < row
    suffix_rhs = (row > col).astype(BF16)

    q = q_ref[0]
    zero = jnp.zeros_like(q)
    q_heads = (jnp.where(first, q, zero), jnp.where(first, zero, q))

    def key_rows(j):
        return pl.ds(pl.multiple_of(j * tile, tile), tile)

    def stage_a(h, j, carry, diag):
        z = _dot(q_heads[h], k_ref[0, key_rows(j), :], ((1,), (1,)))
        t = jnp.log(1.0 + jnp.exp2(-jnp.abs(z))) * LOG2_E
        log_sig = jnp.minimum(z, 0.0) - t
        log_keep = log_sig - z
        if diag:
            log_keep = jnp.where(causal, log_keep, 0.0)
        hi, lo = _split_bf16(log_keep)
        suffix = _dot(hi, suffix_rhs) + _dot(lo, suffix_rhs)
        e = (log_sig + suffix) + carry
        if diag:
            e = jnp.where(causal, e, MASKED_LOG)
        e_s[h] = e
        return carry + (suffix[:, 0:1] + log_keep[:, 0:1])

    def stage_b(h, j):
        a = jnp.exp2(e_s[h]).astype(BF16)
        acc_s[h] += _dot(a, v_ref[0, key_rows(j), :])

    acc_s[...] = jnp.zeros_like(acc_s)
    carry0 = jnp.zeros((tile, 1), F32)
    carries = tuple(stage_a(h, i, carry0, True) for h in range(2))

    def body(jj, carries):
        j = i - jj
        for h in range(2):
            stage_b(h, j)
        return tuple(stage_a(h, j - 1, carries[h], False) for h in range(2))

    lax.fori_loop(0, i, body, carries)
    for h in range(2):
        stage_b(h, 0)
    o = jnp.where(first, acc_s[0], acc_s[1])
    o_ref[0] = (o * _silu(gate_ref[0])).astype(BF16)


def _sb_attention(qkv, rest, batch, seq):
    pairs = SB_WIDTH // LANES
    qkv3 = qkv.reshape(batch, seq, QKV_WIDTH)
    rest3 = rest.reshape(batch, seq, REST_WIDTH)
    return pl.pallas_call(
        _sb_kernel,
        out_shape=jax.ShapeDtypeStruct((batch, seq, SB_WIDTH), BF16),
        grid=(batch, pairs, seq // SB_TILE),
        in_specs=[
            pl.BlockSpec((1, SB_TILE, LANES), lambda b, p, i: (b, i, p)),
            pl.BlockSpec((1, seq, LANES), lambda b, p, i: (b, 0, pairs + p)),
            pl.BlockSpec((1, seq, LANES), lambda b, p, i: (b, 0, 2 * pairs + p)),
            pl.BlockSpec((1, SB_TILE, LANES), lambda b, p, i: (b, i, REST_SB_GATE + p)),
        ],
        out_specs=pl.BlockSpec((1, SB_TILE, LANES), lambda b, p, i: (b, i, p)),
        scratch_shapes=[
            pltpu.VMEM((2, SB_TILE, SB_TILE), F32),
            pltpu.VMEM((2, SB_TILE, LANES), F32),
        ],
        compiler_params=pltpu.CompilerParams(
            dimension_semantics=("arbitrary", "arbitrary", "arbitrary"),
            vmem_limit_bytes=VMEM_LIMIT),
        name="sb_attn",
    )(qkv3, qkv3, qkv3, rest3)


def _shift_rows(x, s, row):
    return jnp.where(row >= s, pltpu.roll(x, s, 0), 0.0)


def _gdn_kernel(q_ref, k_ref, v_ref, gate_ref, ab_ref, cwq_ref, cwk_ref, cwv_ref,
                alog_ref, dtb_ref, nw_ref, o_ref,
                qn_s, kn_s, vc_s, g_s, gl_s, beta_s, n_s, b_s, qh_s, gam_s, o_s):
    seq = q_ref.shape[1]
    n_groups = seq // GDN_GROUP
    n_chunks = seq // GDN_CHUNK
    dk = GDN_HEAD_DIM
    row = lax.broadcasted_iota(jnp.int32, (seq, LANES), 0)

    def conv_silu(x_ref, w_ref):
        x = x_ref[0]
        w = w_ref[...]
        y = x * w[CONV_K - 1:CONV_K, :]
        for s in range(1, CONV_K):
            y = y + _shift_rows(x, s, row) * w[CONV_K - 1 - s:CONV_K - s, :]
        return _silu(y)

    def l2norm(x):
        return x * lax.rsqrt(jnp.sum(x * x, axis=-1, keepdims=True) + EPS)

    qn_s[...] = l2norm(conv_silu(q_ref, cwq_ref)) * (dk ** -0.5)
    kn_s[...] = l2norm(conv_silu(k_ref, cwk_ref))
    vc_s[...] = conv_silu(v_ref, cwv_ref)

    ab = ab_ref[0]
    g_raw = -jnp.exp(alog_ref[0]) * _softplus(ab + dtb_ref[0])
    in_chunk = row % GDN_CHUNK
    g = g_raw
    s = 1
    while s < GDN_CHUNK:
        g = g + jnp.where(in_chunk >= s, pltpu.roll(g, s, 0), 0.0)
        s *= 2
    g_col = jnp.broadcast_to(g[:, 0:1], (seq, LANES))
    g_s[...] = g_col
    g3 = g_col.reshape(n_chunks, GDN_CHUNK, LANES)
    gl_s[...] = jnp.broadcast_to(g3[:, GDN_CHUNK - 1:GDN_CHUNK, :], g3.shape).reshape(seq, LANES)
    beta_s[...] = jnp.broadcast_to(_sigmoid(ab[:, 1:2]), (seq, LANES))

    gi = lax.broadcasted_iota(jnp.int32, (GDN_GROUP, GDN_GROUP), 0)
    gj = lax.broadcasted_iota(jnp.int32, (GDN_GROUP, GDN_GROUP), 1)
    same = (gi // GDN_CHUNK) == (gj // GDN_CHUNK)
    lower = same & (gi >= gj)
    strict = same & (gi > gj)
    eye = (gi == gj).astype(F32)
    lane_chunk = lax.broadcasted_iota(jnp.int32, (dk, GDN_GROUP), 1) // GDN_CHUNK

    def group(r, _):
        r0 = pl.multiple_of(r * GDN_GROUP, GDN_GROUP)
        rows = pl.ds(r0, GDN_GROUP)
        kg = kn_s[rows, :]
        qg = qn_s[rows, :]
        vg = vc_s[rows, :]
        gc = g_s[rows, :]
        gl = gl_s[rows, :]
        bc = beta_s[rows, :]
        g_rowvec = gc.T[0:1, :]
        gcc = jnp.concatenate([gc, gc], axis=1)
        bcc = jnp.concatenate([bc, bc], axis=1)
        dlog = gcc - g_rowvec
        decay = jnp.where(lower, jnp.exp(jnp.where(lower, dlog, 0.0)), 0.0)
        kk = _dot3(kg, kg, ((1,), (1,)))
        x = jnp.where(strict, -(bcc * kk * decay), 0.0)
        p = eye + x
        y = x
        for _ in range(5):
            y = _dot3(y, y)
            p = p + _dot3(p, y)
        rhs = jnp.concatenate([vg * bc, kg * (bc * jnp.exp(gc))], axis=1)
        uw = _dot3(p, rhs)
        qk = _dot(qg.astype(BF16), kg.astype(BF16), ((1,), (1,)))
        intra = jnp.where(lower, qk * decay, 0.0)
        iuw = _dot3(intra, uw)
        o_s[rows, :] = iuw[:, :dk]
        qh_s[rows, :] = (qg * jnp.exp(gc) - iuw[:, dk:]).astype(BF16)
        k_til_t = (kg * jnp.exp(gl - gc)).T
        for c in range(CHUNKS_PER_GROUP):
            bn = _dot3(jnp.where(lane_chunk == c, k_til_t, 0.0), uw)
            idx = r * CHUNKS_PER_GROUP + c
            b_s[idx] = bn[:, :dk]
            n_s[idx] = (-bn[:, dk:]).astype(BF16)
            gam_s[idx] = jnp.exp(gl[c * GDN_CHUNK:c * GDN_CHUNK + 8, :])
        return 0

    lax.fori_loop(0, n_groups, group, 0)

    def scan(c, state):
        rows = pl.ds(pl.multiple_of(c * GDN_CHUNK, GDN_CHUNK), GDN_CHUNK)
        sb = state.astype(BF16)
        o_s[rows, :] = o_s[rows, :] + _dot(qh_s[rows, :], sb)
        gam = gam_s[c][0:1, :]
        return state * gam + (_dot(n_s[c], sb) + b_s[c])

    lax.fori_loop(0, n_chunks, scan, jnp.zeros((dk, dk), F32))

    o = o_s[...]
    ms = jnp.mean(o * o, axis=-1, keepdims=True)
    o = o * lax.rsqrt(ms + EPS) * nw_ref[...]
    o_ref[0] = (o * _silu(gate_ref[0])).astype(BF16)


def _gdn(rest, conv_w, alog_b, dtb_b, norm_w, batch, seq):
    rest3 = rest.reshape(batch, seq, REST_WIDTH)
    n_chunks = seq // GDN_CHUNK
    dk = GDN_HEAD_DIM

    def seq_spec(base):
        return pl.BlockSpec((1, seq, LANES), lambda b, h: (b, 0, base + h))

    def conv_spec(base):
        return pl.BlockSpec((CONV_K, LANES), lambda b, h: (0, base + h))

    head_vec = pl.BlockSpec((1, 1, LANES), lambda b, h: (h, 0, 0))
    return pl.pallas_call(
        _gdn_kernel,
        out_shape=jax.ShapeDtypeStruct((batch, seq, GDN_WIDTH), BF16),
        grid=(batch, GDN_HEADS),
        in_specs=[
            seq_spec(REST_GDN_Q), seq_spec(REST_GDN_K), seq_spec(REST_GDN_V),
            seq_spec(REST_GDN_GATE), seq_spec(REST_AB),
            conv_spec(0), conv_spec(GDN_HEADS), conv_spec(2 * GDN_HEADS),
            head_vec, head_vec,
            pl.BlockSpec((1, LANES), lambda b, h: (0, 0)),
        ],
        out_specs=pl.BlockSpec((1, seq, LANES), lambda b, h: (b, 0, h)),
        scratch_shapes=[
            pltpu.VMEM((seq, dk), F32),
            pltpu.VMEM((seq, dk), F32),
            pltpu.VMEM((seq, dk), F32),
            pltpu.VMEM((seq, LANES), F32),
            pltpu.VMEM((seq, LANES), F32),
            pltpu.VMEM((seq, LANES), F32),
            pltpu.VMEM((n_chunks, dk, dk), BF16),
            pltpu.VMEM((n_chunks, dk, dk), F32),
            pltpu.VMEM((seq, dk), BF16),
            pltpu.VMEM((n_chunks, 8, LANES), F32),
            pltpu.VMEM((seq, dk), F32),
        ],
        compiler_params=pltpu.CompilerParams(
            dimension_semantics=("arbitrary", "arbitrary"), vmem_limit_bytes=VMEM_LIMIT),
        name="gdn",
    )(rest3, rest3, rest3, rest3, rest3, conv_w, conv_w, conv_w, alog_b, dtb_b, norm_w)


def _out_proj_kernel(x_ref, osb_ref, og_ref, msb_ref, mg_ref, wsb_ref, wg_ref, wo_ref, pw_ref, o_ref):
    ya = _dot(osb_ref[...], wsb_ref[...])
    yb = _dot(og_ref[...], wg_ref[...])
    merged = _sigmoid(msb_ref[...]) * ya + _sigmoid(mg_ref[...]) * yb
    y = _dot(merged.astype(BF16), wo_ref[...])
    ms = jnp.mean(y * y, axis=-1, keepdims=True)
    o_ref[...] = x_ref[...] + y * lax.rsqrt(ms + EPS) * pw_ref[...]


def _out_proj(x2d, o_sb, o_g, rest, w_sb, w_g, w_o, post_w, *, tm=256):
    n = x2d.shape[0]
    const = lambda i: (0, 0)
    merge_blocks = D_MODEL // LANES
    return pl.pallas_call(
        _out_proj_kernel,
        out_shape=jax.ShapeDtypeStruct((n, D_MODEL), F32),
        grid=(n // tm,),
        in_specs=[
            pl.BlockSpec((tm, D_MODEL), lambda i: (i, 0)),
            pl.BlockSpec((tm, SB_WIDTH), lambda i: (i, 0)),
            pl.BlockSpec((tm, GDN_WIDTH), lambda i: (i, 0)),
            pl.BlockSpec((tm, D_MODEL), lambda i: (i, REST_MERGE_SB // merge_blocks)),
            pl.BlockSpec((tm, D_MODEL), lambda i: (i, REST_MERGE_GDN // merge_blocks)),
            pl.BlockSpec((SB_WIDTH, D_MODEL), const),
            pl.BlockSpec((GDN_WIDTH, D_MODEL), const),
            pl.BlockSpec((D_MODEL, D_MODEL), const),
            pl.BlockSpec((1, D_MODEL), const),
        ],
        out_specs=pl.BlockSpec((tm, D_MODEL), lambda i: (i, 0)),
        compiler_params=pltpu.CompilerParams(
            dimension_semantics=("arbitrary",), vmem_limit_bytes=VMEM_LIMIT),
        name="out_proj",
    )(x2d, o_sb, o_g, rest, rest, w_sb, w_g, w_o, post_w)


def _regroup_w_in(w):
    o_sb_gate = QKV_WIDTH
    o_gdn_qkv = o_sb_gate + SB_WIDTH
    o_gdn_gate = o_gdn_qkv + 3 * GDN_WIDTH
    o_a = o_gdn_gate + GDN_WIDTH
    o_b = o_a + GDN_HEADS
    o_merge = o_b + GDN_HEADS
    wa = jnp.concatenate(
        [w[:, :SB_WIDTH] * (SB_HEAD_DIM ** -0.5 * LOG2_E), w[:, SB_WIDTH:QKV_WIDTH]], axis=1)
    ab_blocks = []
    for h in range(GDN_HEADS):
        ab_blocks.append(w[:, o_a + h:o_a + h + 1])
        ab_blocks.append(w[:, o_b + h:o_b + h + 1])
        ab_blocks.append(jnp.zeros((w.shape[0], LANES - 2), w.dtype))
    wb = jnp.concatenate([w[:, o_merge:], w[:, o_sb_gate:o_a]] + ab_blocks, axis=1)
    return wa.astype(BF16), wb.astype(BF16)


def kernel(x, pre_norm_w, post_norm_w, w_in, conv_w, a_log, dt_bias, gdn_norm_w,
           w_branch_sb, w_branch_gdn, w_out):
    batch, seq, d = x.shape
    depth = w_in.shape[0]
    assert d == D_MODEL and seq % GDN_GROUP == 0 and seq % SB_TILE == 0 and (batch * seq) % 256 == 0
    x2d = x.reshape(batch * seq, d)
    for layer in range(depth):
        wa, wb = _regroup_w_in(w_in[layer])
        alog_b = jnp.broadcast_to(a_log[layer].astype(F32)[:, None, None], (GDN_HEADS, 1, LANES))
        dtb_b = jnp.broadcast_to(dt_bias[layer].astype(F32)[:, None, None], (GDN_HEADS, 1, LANES))
        qkv, rest = _in_proj(x2d, pre_norm_w[layer][None, :], wa, wb)
        o_sb = _sb_attention(qkv, rest, batch, seq)
        o_g = _gdn(rest, conv_w[layer], alog_b, dtb_b, gdn_norm_w[layer][None, :], batch, seq)
        x2d = _out_proj(x2d, o_sb.reshape(batch * seq, SB_WIDTH), o_g.reshape(batch * seq, GDN_WIDTH),
                        rest, w_branch_sb[layer].astype(BF16), w_branch_gdn[layer].astype(BF16),
                        w_out[layer].astype(BF16), post_norm_w[layer][None, :])
    return x2d.reshape(batch, seq, d)
```

```python
import functools

import jax
import jax.numpy as jnp
from jax import lax
from jax.experimental import pallas as pl
from jax.experimental.pallas import tpu as pltpu

F32 = jnp.float32
BF16 = jnp.bfloat16

D_MODEL = 1024
SB_HEADS = 8
SB_HEAD_DIM = 64
SB_WIDTH = SB_HEADS * SB_HEAD_DIM
GDN_HEADS = 4
GDN_HEAD_DIM = 128
GDN_WIDTH = GDN_HEADS * GDN_HEAD_DIM
GDN_CHUNK = 64
CONV_K = 4
EPS = 1e-6

LANES = 128
MXU_EDGE = 256
SB_TILE = MXU_EDGE
SB_PAIRS_PER_STEP = 2
GDN_GROUP = MXU_EDGE
CHUNKS_PER_GROUP = GDN_GROUP // GDN_CHUNK
GDN_HEADS_PER_STEP = 2
GDN_GROUP_UNROLL = 4
GDN_BASE_BLOCK = 8

REST_MERGE_SB = 0
REST_MERGE_GDN = 8
REST_SB_GATE = 16
REST_GDN_Q = 20
REST_GDN_K = 24
REST_GDN_V = 28
REST_GDN_GATE = 32
REST_AB = 36
REST_WIDTH = (REST_AB + GDN_HEADS) * LANES
QKV_WIDTH = 3 * SB_WIDTH

VMEM_LIMIT = 56 * 1024 * 1024

LOG2_E = 1.4426950408889634
MASKED_LOG = -1e30


def _split_bf16(x):
    hi = x.astype(BF16)
    lo = (x - hi.astype(F32)).astype(BF16)
    return hi, lo


def _dot(a, b, dims=None):
    if dims is None:
        return jnp.dot(a, b, preferred_element_type=F32)
    return lax.dot_general(a, b, (dims, ((), ())), preferred_element_type=F32)


def _sigmoid(x):
    return 1.0 / (1.0 + jnp.exp(-x))


def _silu(x):
    return x * _sigmoid(x)


def _softplus(x):
    return jnp.maximum(x, 0.0) + jnp.log(1.0 + jnp.exp(-jnp.abs(x)))


def _in_proj_kernel(x_ref, pw_ref, wa_ref, wb_ref, qkv_ref, rest_ref, *, col_chunk):
    x = x_ref[...]
    ms = jnp.mean(x * x, axis=-1, keepdims=True)
    h = (x * lax.rsqrt(ms + EPS) * pw_ref[...]).astype(BF16)
    for c in range(QKV_WIDTH // col_chunk):
        sl = slice(c * col_chunk, (c + 1) * col_chunk)
        qkv_ref[:, sl] = _dot(h, wa_ref[:, sl]).astype(BF16)
    for c in range(REST_WIDTH // col_chunk):
        sl = slice(c * col_chunk, (c + 1) * col_chunk)
        rest_ref[:, sl] = _dot(h, wb_ref[:, sl])


def _in_proj(x2d, pre_w, wa, wb, *, tm=256):
    n = x2d.shape[0]
    const = lambda i: (0, 0)
    return pl.pallas_call(
        functools.partial(_in_proj_kernel, col_chunk=512),
        out_shape=(jax.ShapeDtypeStruct((n, QKV_WIDTH), BF16),
                   jax.ShapeDtypeStruct((n, REST_WIDTH), F32)),
        grid=(n // tm,),
        in_specs=[
            pl.BlockSpec((tm, D_MODEL), lambda i: (i, 0)),
            pl.BlockSpec((1, D_MODEL), const),
            pl.BlockSpec((D_MODEL, QKV_WIDTH), const, pipeline_mode=pl.Buffered(1)),
            pl.BlockSpec((D_MODEL, REST_WIDTH), const, pipeline_mode=pl.Buffered(1)),
        ],
        out_specs=(pl.BlockSpec((tm, QKV_WIDTH), lambda i: (i, 0)),
                   pl.BlockSpec((tm, REST_WIDTH), lambda i: (i, 0))),
        compiler_params=pltpu.CompilerParams(
            dimension_semantics=("arbitrary",), vmem_limit_bytes=VMEM_LIMIT),
        name="in_proj",
    )(x2d, pre_w, wa, wb)


def _sb_kernel(q_ref, k_ref, v_ref, gate_ref, o_ref, e_s, acc_s):
    i = pl.program_id(2)
    tile = SB_TILE
    n_heads = 2 * SB_PAIRS_PER_STEP
    lane = lax.broadcasted_iota(jnp.int32, (tile, LANES), 1)
    row = lax.broadcasted_iota(jnp.int32, (tile, tile), 0)
    col = lax.broadcasted_iota(jnp.int32, (tile, tile), 1)
    first = lane < SB_HEAD_DIM
    causal = col < row
    suffix_rhs = (row > col).astype(BF16)

    def pair_lanes(h):
        return slice((h // 2) * LANES, (h // 2 + 1) * LANES)

    q_heads = []
    for h in range(n_heads):
        q = q_ref[0, :, pair_lanes(h)]
        keep = first if h % 2 == 0 else jnp.logical_not(first)
        q_heads.append(jnp.where(keep, q, jnp.zeros_like(q)))

    def key_rows(j):
        return pl.ds(pl.multiple_of(j * tile, tile), tile)

    heads = range(n_heads)

    def stage_a(j, carries, diag):
        z = [_dot(q_heads[h], k_ref[0, key_rows(j), pair_lanes(h)], ((1,), (1,))) for h in heads]
        t = [jnp.log(1.0 + jnp.exp2(-jnp.abs(v))) * LOG2_E for v in z]
        log_sig = [jnp.minimum(z[h], 0.0) - t[h] for h in heads]
        log_keep = [log_sig[h] - z[h] for h in heads]
        if diag:
            log_keep = [jnp.where(causal, v, 0.0) for v in log_keep]
        suffix = [_dot(v.astype(BF16), suffix_rhs) for v in log_keep]
        for h in heads:
            e = (log_sig[h] + suffix[h]) + carries[h]
            if diag:
                e = jnp.where(causal, e, MASKED_LOG)
            e_s[h] = e
        return tuple(carries[h] + (suffix[h][:, 0:1] + log_keep[h][:, 0:1]) for h in heads)

    def stage_b(j):
        a = [jnp.exp2(e_s[h]).astype(BF16) for h in heads]
        for h in heads:
            acc_s[h] += _dot(a[h], v_ref[0, key_rows(j), pair_lanes(h)])

    acc_s[...] = jnp.zeros_like(acc_s)
    carry0 = jnp.zeros((tile, 1), F32)
    carries = stage_a(i, (carry0,) * n_heads, True)

    def body(jj, carries):
        j = i - jj
        stage_b(j)
        return stage_a(j - 1, carries, False)

    lax.fori_loop(0, i, body, carries)
    stage_b(0)
    o = jnp.concatenate(
        [jnp.where(first, acc_s[2 * p], acc_s[2 * p + 1]) for p in range(SB_PAIRS_PER_STEP)], axis=1)
    o_ref[0] = (o * _silu(gate_ref[0])).astype(BF16)


def _sb_attention(qkv, rest, batch, seq):
    width = SB_PAIRS_PER_STEP * LANES
    steps = SB_WIDTH // width
    qkv3 = qkv.reshape(batch, seq, QKV_WIDTH)
    rest3 = rest.reshape(batch, seq, REST_WIDTH)
    gate_base = REST_SB_GATE // SB_PAIRS_PER_STEP
    return pl.pallas_call(
        _sb_kernel,
        out_shape=jax.ShapeDtypeStruct((batch, seq, SB_WIDTH), BF16),
        grid=(batch, steps, seq // SB_TILE),
        in_specs=[
            pl.BlockSpec((1, SB_TILE, width), lambda b, p, i: (b, i, p)),
            pl.BlockSpec((1, seq, width), lambda b, p, i: (b, 0, steps + p)),
            pl.BlockSpec((1, seq, width), lambda b, p, i: (b, 0, 2 * steps + p)),
            pl.BlockSpec((1, SB_TILE, width), lambda b, p, i: (b, i, gate_base + p)),
        ],
        out_specs=pl.BlockSpec((1, SB_TILE, width), lambda b, p, i: (b, i, p)),
        scratch_shapes=[
            pltpu.VMEM((2 * SB_PAIRS_PER_STEP, SB_TILE, SB_TILE), F32),
            pltpu.VMEM((2 * SB_PAIRS_PER_STEP, SB_TILE, LANES), F32),
        ],
        compiler_params=pltpu.CompilerParams(
            dimension_semantics=("arbitrary", "arbitrary", "arbitrary"),
            vmem_limit_bytes=VMEM_LIMIT),
        name="sb_attn",
    )(qkv3, qkv3, qkv3, rest3)


def _shift_rows(x, s, row):
    return jnp.where(row >= s, pltpu.roll(x, s, 0), 0.0)


def _gdn_kernel(q_ref, k_ref, v_ref, gate_ref, ab_ref, cwq_ref, cwk_ref, cwv_ref,
                alog_ref, dtb_ref, nw_ref, o_ref,
                qn_s, kn_s, vc_s, g_s, gl_s, beta_s, n_s, b_s, qh_s, gam_s, o_s):
    seq = q_ref.shape[1]
    n_groups = seq // GDN_GROUP
    n_chunks = seq // GDN_CHUNK
    unroll = min(GDN_GROUP_UNROLL, n_groups)
    dk = GDN_HEAD_DIM
    row = lax.broadcasted_iota(jnp.int32, (seq, LANES), 0)
    in_chunk = row % GDN_CHUNK

    gi = lax.broadcasted_iota(jnp.int32, (GDN_GROUP, GDN_GROUP), 0)
    gj = lax.broadcasted_iota(jnp.int32, (GDN_GROUP, GDN_GROUP), 1)
    same = (gi // GDN_CHUNK) == (gj // GDN_CHUNK)
    lower = same & (gi >= gj)
    strict = same & (gi > gj)
    eye = (gi == gj).astype(F32)
    base_blocks = (gi // GDN_BASE_BLOCK) == (gj // GDN_BASE_BLOCK)
    joins = []
    s = GDN_BASE_BLOCK
    while s < GDN_CHUNK:
        joins.append(((gi // (2 * s)) == (gj // (2 * s))) & ((gi // s) != (gj // s)))
        s *= 2
    lane_chunk = lax.broadcasted_iota(jnp.int32, (dk, GDN_GROUP), 1) // GDN_CHUNK

    def conv_silu(x, w):
        y = x * w[CONV_K - 1:CONV_K, :]
        for s in range(1, CONV_K):
            y = y + _shift_rows(x, s, row) * w[CONV_K - 1 - s:CONV_K - s, :]
        return _silu(y)

    def l2norm(x):
        return x * lax.rsqrt(jnp.sum(x * x, axis=-1, keepdims=True) + EPS)

    def prologue(head):
        lanes = slice(head * LANES, (head + 1) * LANES)
        qn_s[...] = l2norm(conv_silu(q_ref[0, :, lanes], cwq_ref[:, lanes])) * (dk ** -0.5)
        kn_s[...] = l2norm(conv_silu(k_ref[0, :, lanes], cwk_ref[:, lanes]))
        vc_s[...] = conv_silu(v_ref[0, :, lanes], cwv_ref[:, lanes])
        ab = ab_ref[0, :, lanes]
        g = -jnp.exp(alog_ref[head]) * _softplus(ab + dtb_ref[head])
        s = 1
        while s < GDN_CHUNK:
            g = g + jnp.where(in_chunk >= s, pltpu.roll(g, s, 0), 0.0)
            s *= 2
        g_col = jnp.broadcast_to(g[:, 0:1], (seq, LANES))
        g_s[...] = g_col
        g3 = g_col.reshape(n_chunks, GDN_CHUNK, LANES)
        gl_s[...] = jnp.broadcast_to(g3[:, GDN_CHUNK - 1:GDN_CHUNK, :], g3.shape).reshape(seq, LANES)
        beta_s[...] = jnp.broadcast_to(_sigmoid(ab[:, 1:2]), (seq, LANES))

    def groups(head, rs):
        each = range(len(rs))
        rows = [pl.ds(pl.multiple_of(r * GDN_GROUP, GDN_GROUP), GDN_GROUP) for r in rs]
        kg = [kn_s[rw, :] for rw in rows]
        gc = [g_s[rw, :] for rw in rows]
        bc = [beta_s[rw, :] for rw in rows]
        kgb = [k.astype(BF16) for k in kg]
        kk = [_dot(k, k, ((1,), (1,))) for k in kgb]
        decay = []
        for g in gc:
            g_rowvec = g.T[0:1, :]
            dlog = jnp.concatenate([g, g], axis=1) - g_rowvec
            decay.append(jnp.where(lower, jnp.exp(jnp.where(lower, dlog, 0.0)), 0.0))
        x = [jnp.where(strict, -(jnp.concatenate([bc[u], bc[u]], axis=1) * kk[u] * decay[u]), 0.0)
             for u in each]
        xb = [v.astype(BF16) for v in x]
        y = [jnp.where(base_blocks, v, 0.0) for v in x]
        p = [eye + v for v in y]
        for _ in range(GDN_BASE_BLOCK.bit_length() - 2):
            yb = [v.astype(BF16) for v in y]
            y = [_dot(v, v) for v in yb]
            p = [p[u] + _dot(p[u].astype(BF16), y[u].astype(BF16)) for u in each]
        for join in joins:
            pb = [v.astype(BF16) for v in p]
            t = [_dot(jnp.where(join, xb[u], jnp.zeros_like(xb[u])), pb[u]) for u in each]
            p = [p[u] + _dot(pb[u], t[u].astype(BF16)) for u in each]
        vg = [vc_s[rw, :] for rw in rows]
        rhs = [jnp.concatenate([vg[u] * bc[u], kg[u] * (bc[u] * jnp.exp(gc[u]))], axis=1) for u in each]
        uw = [_dot(p[u].astype(BF16), rhs[u].astype(BF16)) for u in each]
        uwb = [v.astype(BF16) for v in uw]
        qg = [qn_s[rw, :] for rw in rows]
        qk = [_dot(qg[u].astype(BF16), kgb[u], ((1,), (1,))) for u in each]
        intra = [jnp.where(lower, qk[u] * decay[u], 0.0).astype(BF16) for u in each]
        iuw = [_dot(intra[u], uwb[u]) for u in each]
        gl = [gl_s[rw, :] for rw in rows]
        k_til_t = [(kg[u] * jnp.exp(gl[u] - gc[u])).T.astype(BF16) for u in each]
        for u in each:
            o_s[head, rows[u], :] = iuw[u][:, :dk]
            qh_s[head, rows[u], :] = (qg[u] * jnp.exp(gc[u]) - iuw[u][:, dk:]).astype(BF16)
        for c in range(CHUNKS_PER_GROUP):
            for u in each:
                bn = _dot(jnp.where(lane_chunk == c, k_til_t[u], jnp.zeros_like(k_til_t[u])), uwb[u])
                idx = rs[u] * CHUNKS_PER_GROUP + c
                b_s[head, idx] = bn[:, :dk]
                n_s[head, idx] = (-bn[:, dk:]).astype(BF16)
                gam_s[head, idx] = jnp.exp(gl[u][c * GDN_CHUNK:c * GDN_CHUNK + 8, :])

    for head in range(GDN_HEADS_PER_STEP):
        prologue(head)

        def group_iter(it, carry, head=head):
            groups(head, [it * unroll + u for u in range(unroll)])
            return carry

        lax.fori_loop(0, n_groups // unroll, group_iter, 0)

    def scan(c, states):
        rows = pl.ds(pl.multiple_of(c * GDN_CHUNK, GDN_CHUNK), GDN_CHUNK)
        new_states = []
        for head in range(GDN_HEADS_PER_STEP):
            sb = states[head].astype(BF16)
            o_s[head, rows, :] = o_s[head, rows, :] + _dot(qh_s[head, rows, :], sb)
            gam = gam_s[head, c][0:1, :]
            new_states.append(states[head] * gam + (_dot(n_s[head, c], sb) + b_s[head, c]))
        return tuple(new_states)

    state0 = jnp.zeros((dk, dk), F32)
    lax.fori_loop(0, n_chunks, scan, (state0,) * GDN_HEADS_PER_STEP)

    for head in range(GDN_HEADS_PER_STEP):
        lanes = slice(head * LANES, (head + 1) * LANES)
        o = o_s[head]
        ms = jnp.mean(o * o, axis=-1, keepdims=True)
        o = o * lax.rsqrt(ms + EPS) * nw_ref[...]
        o_ref[0, :, lanes] = (o * _silu(gate_ref[0, :, lanes])).astype(BF16)


def _gdn(rest, conv_w, alog_b, dtb_b, norm_w, batch, seq):
    rest3 = rest.reshape(batch, seq, REST_WIDTH)
    n_chunks = seq // GDN_CHUNK
    dk = GDN_HEAD_DIM
    hps = GDN_HEADS_PER_STEP
    width = hps * LANES
    steps = GDN_HEADS // hps

    def seq_spec(base):
        return pl.BlockSpec((1, seq, width), lambda b, h: (b, 0, base // hps + h))

    def conv_spec(base):
        return pl.BlockSpec((CONV_K, width), lambda b, h: (0, base // hps + h))

    head_vec = pl.BlockSpec((hps, 1, LANES), lambda b, h: (h, 0, 0))
    return pl.pallas_call(
        _gdn_kernel,
        out_shape=jax.ShapeDtypeStruct((batch, seq, GDN_WIDTH), BF16),
        grid=(batch, steps),
        in_specs=[
            seq_spec(REST_GDN_Q), seq_spec(REST_GDN_K), seq_spec(REST_GDN_V),
            seq_spec(REST_GDN_GATE), seq_spec(REST_AB),
            conv_spec(0), conv_spec(GDN_HEADS), conv_spec(2 * GDN_HEADS),
            head_vec, head_vec,
            pl.BlockSpec((1, LANES), lambda b, h: (0, 0)),
        ],
        out_specs=pl.BlockSpec((1, seq, width), lambda b, h: (b, 0, h)),
        scratch_shapes=[
            pltpu.VMEM((seq, dk), F32),
            pltpu.VMEM((seq, dk), F32),
            pltpu.VMEM((seq, dk), F32),
            pltpu.VMEM((seq, LANES), F32),
            pltpu.VMEM((seq, LANES), F32),
            pltpu.VMEM((seq, LANES), F32),
            pltpu.VMEM((hps, n_chunks, dk, dk), BF16),
            pltpu.VMEM((hps, n_chunks, dk, dk), F32),
            pltpu.VMEM((hps, seq, dk), BF16),
            pltpu.VMEM((hps, n_chunks, 8, LANES), F32),
            pltpu.VMEM((hps, seq, dk), F32),
        ],
        compiler_params=pltpu.CompilerParams(
            dimension_semantics=("arbitrary", "arbitrary"), vmem_limit_bytes=VMEM_LIMIT),
        name="gdn",
    )(rest3, rest3, rest3, rest3, rest3, conv_w, conv_w, conv_w, alog_b, dtb_b, norm_w)


def _out_proj_kernel(x_ref, osb_ref, og_ref, msb_ref, mg_ref, wsb_ref, wg_ref, wo_ref, pw_ref, o_ref):
    ya = _dot(osb_ref[...], wsb_ref[...])
    yb = _dot(og_ref[...], wg_ref[...])
    merged = _sigmoid(msb_ref[...]) * ya + _sigmoid(mg_ref[...]) * yb
    y = _dot(merged.astype(BF16), wo_ref[...])
    ms = jnp.mean(y * y, axis=-1, keepdims=True)
    o_ref[...] = x_ref[...] + y * lax.rsqrt(ms + EPS) * pw_ref[...]


def _out_proj(x2d, o_sb, o_g, rest, w_sb, w_g, w_o, post_w, *, tm=256):
    n = x2d.shape[0]
    const = lambda i: (0, 0)
    merge_blocks = D_MODEL // LANES
    return pl.pallas_call(
        _out_proj_kernel,
        out_shape=jax.ShapeDtypeStruct((n, D_MODEL), F32),
        grid=(n // tm,),
        in_specs=[
            pl.BlockSpec((tm, D_MODEL), lambda i: (i, 0)),
            pl.BlockSpec((tm, SB_WIDTH), lambda i: (i, 0)),
            pl.BlockSpec((tm, GDN_WIDTH), lambda i: (i, 0)),
            pl.BlockSpec((tm, D_MODEL), lambda i: (i, REST_MERGE_SB // merge_blocks)),
            pl.BlockSpec((tm, D_MODEL), lambda i: (i, REST_MERGE_GDN // merge_blocks)),
            pl.BlockSpec((SB_WIDTH, D_MODEL), const),
            pl.BlockSpec((GDN_WIDTH, D_MODEL), const),
            pl.BlockSpec((D_MODEL, D_MODEL), const),
            pl.BlockSpec((1, D_MODEL), const),
        ],
        out_specs=pl.BlockSpec((tm, D_MODEL), lambda i: (i, 0)),
        compiler_params=pltpu.CompilerParams(
            dimension_semantics=("arbitrary",), vmem_limit_bytes=VMEM_LIMIT),
        name="out_proj",
    )(x2d, o_sb, o_g, rest, rest, w_sb, w_g, w_o, post_w)


def _regroup_w_in(w):
    o_sb_gate = QKV_WIDTH
    o_gdn_qkv = o_sb_gate + SB_WIDTH
    o_gdn_gate = o_gdn_qkv + 3 * GDN_WIDTH
    o_a = o_gdn_gate + GDN_WIDTH
    o_b = o_a + GDN_HEADS
    o_merge = o_b + GDN_HEADS
    wa = jnp.concatenate(
        [w[:, :SB_WIDTH] * (SB_HEAD_DIM ** -0.5 * LOG2_E), w[:, SB_WIDTH:QKV_WIDTH]], axis=1)
    ab_blocks = []
    for h in range(GDN_HEADS):
        ab_blocks.append(w[:, o_a + h:o_a + h + 1])
        ab_blocks.append(w[:, o_b + h:o_b + h + 1])
        ab_blocks.append(jnp.zeros((w.shape[0], LANES - 2), w.dtype))
    wb = jnp.concatenate([w[:, o_merge:], w[:, o_sb_gate:o_a]] + ab_blocks, axis=1)
    return wa.astype(BF16), wb.astype(BF16)


def kernel(x, pre_norm_w, post_norm_w, w_in, conv_w, a_log, dt_bias, gdn_norm_w,
           w_branch_sb, w_branch_gdn, w_out):
    batch, seq, d = x.shape
    depth = w_in.shape[0]
    assert d == D_MODEL and seq % GDN_GROUP == 0 and seq % SB_TILE == 0 and (batch * seq) % 256 == 0
    x2d = x.reshape(batch * seq, d)
    for layer in range(depth):
        wa, wb = _regroup_w_in(w_in[layer])
        alog_b = jnp.broadcast_to(a_log[layer].astype(F32)[:, None, None], (GDN_HEADS, 1, LANES))
        dtb_b = jnp.broadcast_to(dt_bias[layer].astype(F32)[:, None, None], (GDN_HEADS, 1, LANES))
        qkv, rest = _in_proj(x2d, pre_norm_w[layer][None, :], wa, wb)
        o_sb = _sb_attention(qkv, rest, batch, seq)
        o_g = _gdn(rest, conv_w[layer], alog_b, dtb_b, gdn_norm_w[layer][None, :], batch, seq)
        x2d = _out_proj(x2d, o_sb.reshape(batch * seq, SB_WIDTH), o_g.reshape(batch * seq, GDN_WIDTH),
                        rest, w_branch_sb[layer].astype(BF16), w_branch_gdn[layer].astype(BF16),
                        w_out[layer].astype(BF16), post_norm_w[layer][None, :])
    return x2d.reshape(batch, seq, d)
```

```python
import functools

import jax
import jax.numpy as jnp
from jax import lax
from jax.experimental import pallas as pl
from jax.experimental.pallas import tpu as pltpu

F32 = jnp.float32
BF16 = jnp.bfloat16

D_MODEL = 1024
SB_HEADS = 8
SB_HEAD_DIM = 64
SB_WIDTH = SB_HEADS * SB_HEAD_DIM
GDN_HEADS = 4
GDN_HEAD_DIM = 128
GDN_WIDTH = GDN_HEADS * GDN_HEAD_DIM
GDN_CHUNK = 64
CONV_K = 4
EPS = 1e-6

LANES = 128
BF16_ROWS = 16
MXU_EDGE = 256
ROW_TILE = 256
SB_TILE = MXU_EDGE
SB_PAIRS_PER_STEP = 2
GDN_GROUP = MXU_EDGE
CHUNKS_PER_GROUP = GDN_GROUP // GDN_CHUNK
GDN_HEADS_PER_STEP = 4
GDN_GROUP_UNROLL = 4
GDN_BASE_BLOCK = 8

SIDE_MERGE_SB = 0
SIDE_MERGE_GDN = 8
SIDE_SB_GATE = 16
SIDE_GDN_Q = 20
SIDE_GDN_K = 24
SIDE_GDN_V = 28
SIDE_GDN_GATE = 32
SIDE_BLOCKS = 36
SIDE_WIDTH = SIDE_BLOCKS * LANES
AB_WIDTH = LANES
QKV_WIDTH = 3 * SB_WIDTH
COL_CHUNK = 512

VMEM_LIMIT = 60 * 1024 * 1024

LOG2_E = 1.4426950408889634
MASKED_LOG = -1e30


def _dot(a, b, dims=None):
    if dims is None:
        return jnp.dot(a, b, preferred_element_type=F32)
    return lax.dot_general(a, b, (dims, ((), ())), preferred_element_type=F32)


def _sigmoid(x):
    return 1.0 / (1.0 + jnp.exp(-x))


def _silu(x):
    return x * _sigmoid(x)


def _softplus(x):
    return jnp.maximum(x, 0.0) + jnp.log(1.0 + jnp.exp(-jnp.abs(x)))


def _in_proj_kernel(x_ref, pw_ref, wa_ref, wb_ref, wab_ref, qkv_ref, side_ref, ab_ref):
    x = x_ref[...]
    ms = jnp.mean(x * x, axis=-1, keepdims=True)
    h = (x * lax.rsqrt(ms + EPS) * pw_ref[...]).astype(BF16)
    per_chunk = COL_CHUNK // LANES
    for c in range(QKV_WIDTH // COL_CHUNK):
        sl = slice(c * COL_CHUNK, (c + 1) * COL_CHUNK)
        qkv_ref[:, sl] = _dot(h, wa_ref[:, sl]).astype(BF16)
    for c in range(SIDE_WIDTH // COL_CHUNK):
        res = _dot(h, wb_ref[:, c * COL_CHUNK:(c + 1) * COL_CHUNK]).astype(BF16)
        for k in range(per_chunk):
            side_ref[c * per_chunk + k] = res[:, k * LANES:(k + 1) * LANES]
    ab_ref[...] = _dot(h, wab_ref[...])


def _in_proj(x2d, pre_w, wa, wb, wab):
    n = x2d.shape[0]
    tm = ROW_TILE
    const = lambda i: (0, 0)
    resident = functools.partial(pl.BlockSpec, index_map=const, pipeline_mode=pl.Buffered(1))
    return pl.pallas_call(
        _in_proj_kernel,
        out_shape=(jax.ShapeDtypeStruct((n, QKV_WIDTH), BF16),
                   jax.ShapeDtypeStruct((SIDE_BLOCKS, n, LANES), BF16),
                   jax.ShapeDtypeStruct((n, AB_WIDTH), F32)),
        grid=(n // tm,),
        in_specs=[
            pl.BlockSpec((tm, D_MODEL), lambda i: (i, 0)),
            pl.BlockSpec((1, D_MODEL), const),
            resident((D_MODEL, QKV_WIDTH)),
            resident((D_MODEL, SIDE_WIDTH)),
            resident((D_MODEL, AB_WIDTH)),
        ],
        out_specs=(pl.BlockSpec((tm, QKV_WIDTH), lambda i: (i, 0)),
                   pl.BlockSpec((SIDE_BLOCKS, tm, LANES), lambda i: (0, i, 0)),
                   pl.BlockSpec((tm, AB_WIDTH), lambda i: (i, 0))),
        compiler_params=pltpu.CompilerParams(
            dimension_semantics=("arbitrary",), vmem_limit_bytes=VMEM_LIMIT),
        name="in_proj",
    )(x2d, pre_w, wa, wb, wab)


def _sb_kernel(q_ref, k_ref, v_ref, gate_ref, o_ref, e_s, acc_s):
    i = pl.program_id(2)
    tile = SB_TILE
    n_heads = 2 * SB_PAIRS_PER_STEP
    heads = range(n_heads)
    lane = lax.broadcasted_iota(jnp.int32, (tile, LANES), 1)
    row = lax.broadcasted_iota(jnp.int32, (tile, tile), 0)
    col = lax.broadcasted_iota(jnp.int32, (tile, tile), 1)
    first = lane < SB_HEAD_DIM
    causal = col < row
    suffix_rhs = (row > col).astype(BF16)

    def pair_lanes(h):
        return slice((h // 2) * LANES, (h // 2 + 1) * LANES)

    q_heads = []
    for h in heads:
        q = q_ref[0, :, pair_lanes(h)]
        keep = first if h % 2 == 0 else jnp.logical_not(first)
        q_heads.append(jnp.where(keep, q, jnp.zeros_like(q)))

    def key_rows(j):
        return pl.ds(pl.multiple_of(j * tile, tile), tile)

    def stage_a(j, carries, diag):
        z = [_dot(q_heads[h], k_ref[0, key_rows(j), pair_lanes(h)], ((1,), (1,))) for h in heads]
        t = [jnp.log(1.0 + jnp.exp2(-jnp.abs(v))) * LOG2_E for v in z]
        log_sig = [jnp.minimum(z[h], 0.0) - t[h] for h in heads]
        log_keep = [log_sig[h] - z[h] for h in heads]
        if diag:
            log_keep = [jnp.where(causal, v, 0.0) for v in log_keep]
        suffix = [_dot(v.astype(BF16), suffix_rhs) for v in log_keep]
        for h in heads:
            e = (log_sig[h] + suffix[h]) + carries[h]
            if diag:
                e = jnp.where(causal, e, MASKED_LOG)
            e_s[h] = e
        return tuple(carries[h] + (suffix[h][:, 0:1] + log_keep[h][:, 0:1]) for h in heads)

    def stage_b(j):
        a = [jnp.exp2(e_s[h]).astype(BF16) for h in heads]
        for h in heads:
            acc_s[h] += _dot(a[h], v_ref[0, key_rows(j), pair_lanes(h)])

    acc_s[...] = jnp.zeros_like(acc_s)
    carry0 = jnp.zeros((tile, 1), F32)
    carries = stage_a(i, (carry0,) * n_heads, True)

    def body(jj, carries):
        j = i - jj
        stage_b(j)
        return stage_a(j - 1, carries, False)

    lax.fori_loop(0, i, body, carries)
    stage_b(0)
    o = jnp.concatenate(
        [jnp.where(first, acc_s[2 * p], acc_s[2 * p + 1]) for p in range(SB_PAIRS_PER_STEP)], axis=1)
    gate = jnp.concatenate([gate_ref[p, 0] for p in range(SB_PAIRS_PER_STEP)], axis=1).astype(F32)
    o_ref[0] = (o * _silu(gate)).astype(BF16)


def _sb_attention(qkv, side, batch, seq):
    width = SB_PAIRS_PER_STEP * LANES
    steps = SB_WIDTH // width
    qkv3 = qkv.reshape(batch, seq, QKV_WIDTH)
    side4 = side.reshape(SIDE_BLOCKS, batch, seq, LANES)
    gate_base = SIDE_SB_GATE // SB_PAIRS_PER_STEP
    return pl.pallas_call(
        _sb_kernel,
        out_shape=jax.ShapeDtypeStruct((batch, seq, SB_WIDTH), BF16),
        grid=(batch, steps, seq // SB_TILE),
        in_specs=[
            pl.BlockSpec((1, SB_TILE, width), lambda b, p, i: (b, i, p)),
            pl.BlockSpec((1, seq, width), lambda b, p, i: (b, 0, steps + p)),
            pl.BlockSpec((1, seq, width), lambda b, p, i: (b, 0, 2 * steps + p)),
            pl.BlockSpec((SB_PAIRS_PER_STEP, 1, SB_TILE, LANES), lambda b, p, i: (gate_base + p, b, i, 0)),
        ],
        out_specs=pl.BlockSpec((1, SB_TILE, width), lambda b, p, i: (b, i, p)),
        scratch_shapes=[
            pltpu.VMEM((2 * SB_PAIRS_PER_STEP, SB_TILE, SB_TILE), F32),
            pltpu.VMEM((2 * SB_PAIRS_PER_STEP, SB_TILE, LANES), F32),
        ],
        compiler_params=pltpu.CompilerParams(
            dimension_semantics=("arbitrary", "arbitrary", "arbitrary"),
            vmem_limit_bytes=VMEM_LIMIT),
        name="sb_attn",
    )(qkv3, qkv3, qkv3, side4)


def _gdn_kernel(q_ref, k_ref, v_ref, gate_ref, ab_ref, cw_ref, alog_ref, dtb_ref, nw_ref, o_ref,
                qa_s, ka_s, va_s, ga_s, gla_s, ba_s, qb_s, kb_s, vb_s, gb_s, glb_s, bb_s,
                conv_s, n_s, b_s, qh_s, gam_s, o_s):
    seq = q_ref.shape[2]
    n_groups = seq // GDN_GROUP
    n_chunks = seq // GDN_CHUNK
    unroll = min(GDN_GROUP_UNROLL, n_groups)
    runs_per_head = n_groups // unroll
    n_items = GDN_HEADS_PER_STEP * runs_per_head
    run_rows = unroll * GDN_GROUP
    dk = GDN_HEAD_DIM
    stage_sets = ((qa_s, ka_s, va_s, ga_s, gla_s, ba_s), (qb_s, kb_s, vb_s, gb_s, glb_s, bb_s))

    in_chunk = lax.broadcasted_iota(jnp.int32, (GDN_GROUP, LANES), 0) % GDN_CHUNK
    ab_lane = lax.broadcasted_iota(jnp.int32, (GDN_GROUP, LANES), 1)
    gi = lax.broadcasted_iota(jnp.int32, (GDN_GROUP, GDN_GROUP), 0)
    gj = lax.broadcasted_iota(jnp.int32, (GDN_GROUP, GDN_GROUP), 1)
    same = (gi // GDN_CHUNK) == (gj // GDN_CHUNK)
    lower = same & (gi >= gj)
    strict = same & (gi > gj)
    eye = (gi == gj).astype(F32)
    base_blocks = (gi // GDN_BASE_BLOCK) == (gj // GDN_BASE_BLOCK)
    joins = []
    s = GDN_BASE_BLOCK
    while s < GDN_CHUNK:
        joins.append(((gi // (2 * s)) == (gj // (2 * s))) & ((gi // s) != (gj // s)))
        s *= 2
    lane_chunk = lax.broadcasted_iota(jnp.int32, (dk, GDN_GROUP), 1) // GDN_CHUNK

    def item_coords(w):
        if runs_per_head == 1:
            return w, 0
        return w // runs_per_head, w % runs_per_head

    def stage(w, refs):
        qn_s, kn_s, vc_s, g_s, gl_s, beta_s = refs
        head, run = item_coords(w)

        def conv_silu(x_ref, section, r0, first_rows):
            cur = x_ref[head, 0, pl.ds(r0, GDN_GROUP), :].astype(F32)
            prev0 = pl.multiple_of(jnp.maximum(r0 - BF16_ROWS, 0), BF16_ROWS)
            prev = x_ref[head, 0, pl.ds(prev0, BF16_ROWS), :].astype(F32)[BF16_ROWS - 8:, :]
            if first_rows:
                prev = jnp.where(r0 > 0, prev, 0.0)
            win = conv_s.at[section]
            win[0:8, :] = prev
            win[8:8 + GDN_GROUP, :] = cur
            w_taps = cw_ref[section, head]
            y = cur * w_taps[CONV_K - 1:CONV_K, :]
            for s in range(1, CONV_K):
                y = y + win[8 - s:8 - s + GDN_GROUP, :] * w_taps[CONV_K - 1 - s:CONV_K - s, :]
            return _silu(y)

        def l2norm(x):
            return x * lax.rsqrt(jnp.sum(x * x, axis=-1, keepdims=True) + EPS)

        for u in range(unroll):
            rw = slice(u * GDN_GROUP, (u + 1) * GDN_GROUP)
            r0 = pl.multiple_of(run * run_rows + u * GDN_GROUP, GDN_GROUP)
            qn_s[rw, :] = l2norm(conv_silu(q_ref, 0, r0, u == 0)) * (dk ** -0.5)
            kn_s[rw, :] = l2norm(conv_silu(k_ref, 1, r0, u == 0))
            vc_s[rw, :] = conv_silu(v_ref, 2, r0, u == 0)
            ab = ab_ref[0, pl.ds(r0, GDN_GROUP), :]
            a_lane = pl.program_id(1) * GDN_HEADS_PER_STEP + head
            a_col = jnp.sum(jnp.where(ab_lane == a_lane, ab, 0.0), axis=-1, keepdims=True)
            b_col = jnp.sum(jnp.where(ab_lane == a_lane + GDN_HEADS, ab, 0.0), axis=-1, keepdims=True)
            a_b = jnp.broadcast_to(a_col, (GDN_GROUP, LANES))
            g = -jnp.exp(alog_ref[head]) * _softplus(a_b + dtb_ref[head])
            s = 1
            while s < GDN_CHUNK:
                g = g + jnp.where(in_chunk >= s, pltpu.roll(g, s, 0), 0.0)
                s *= 2
            g_s[rw, :] = g
            g3 = g.reshape(CHUNKS_PER_GROUP, GDN_CHUNK, LANES)
            gl_s[rw, :] = jnp.broadcast_to(g3[:, GDN_CHUNK - 1:GDN_CHUNK, :], g3.shape).reshape(GDN_GROUP, LANES)
            beta_s[rw, :] = jnp.broadcast_to(_sigmoid(b_col), (GDN_GROUP, LANES))

    def chains(w, refs):
        qn_s, kn_s, vc_s, g_s, gl_s, beta_s = refs
        head, run = item_coords(w)
        each = range(unroll)
        rows = [slice(u * GDN_GROUP, (u + 1) * GDN_GROUP) for u in each]
        kg = [kn_s[rw, :] for rw in rows]
        gc = [g_s[rw, :] for rw in rows]
        bc = [beta_s[rw, :] for rw in rows]
        kgb = [k.astype(BF16) for k in kg]
        kk = [_dot(k, k, ((1,), (1,))) for k in kgb]
        decay = []
        for g in gc:
            g_rowvec = g.T[0:1, :]
            dlog = jnp.concatenate([g, g], axis=1) - g_rowvec
            decay.append(jnp.where(lower, jnp.exp(jnp.where(lower, dlog, 0.0)), 0.0))
        x = [jnp.where(strict, -(jnp.concatenate([bc[u], bc[u]], axis=1) * kk[u] * decay[u]), 0.0)
             for u in each]
        xb = [v.astype(BF16) for v in x]
        y = [jnp.where(base_blocks, v, 0.0) for v in x]
        p = [eye + v for v in y]
        for _ in range(GDN_BASE_BLOCK.bit_length() - 2):
            yb = [v.astype(BF16) for v in y]
            y = [_dot(v, v) for v in yb]
            p = [p[u] + _dot(p[u].astype(BF16), y[u].astype(BF16)) for u in each]
        for join in joins:
            pb = [v.astype(BF16) for v in p]
            t = [_dot(jnp.where(join, xb[u], jnp.zeros_like(xb[u])), pb[u]) for u in each]
            p = [p[u] + _dot(pb[u], t[u].astype(BF16)) for u in each]
        vg = [vc_s[rw, :] for rw in rows]
        rhs = [jnp.concatenate([vg[u] * bc[u], kg[u] * (bc[u] * jnp.exp(gc[u]))], axis=1) for u in each]
        uw = [_dot(p[u].astype(BF16), rhs[u].astype(BF16)) for u in each]
        uwb = [v.astype(BF16) for v in uw]
        qg = [qn_s[rw, :] for rw in rows]
        qk = [_dot(qg[u].astype(BF16), kgb[u], ((1,), (1,))) for u in each]
        intra = [jnp.where(lower, qk[u] * decay[u], 0.0).astype(BF16) for u in each]
        iuw = [_dot(intra[u], uwb[u]) for u in each]
        gl = [gl_s[rw, :] for rw in rows]
        k_til_t = [(kg[u] * jnp.exp(gl[u] - gc[u])).T.astype(BF16) for u in each]
        for u in each:
            out_rows = pl.ds(pl.multiple_of(run * run_rows + u * GDN_GROUP, GDN_GROUP), GDN_GROUP)
            o_s[head, out_rows, :] = iuw[u][:, :dk]
            qh_s[head, out_rows, :] = (qg[u] * jnp.exp(gc[u]) - iuw[u][:, dk:]).astype(BF16)
        for c in range(CHUNKS_PER_GROUP):
            for u in each:
                bn = _dot(jnp.where(lane_chunk == c, k_til_t[u], jnp.zeros_like(k_til_t[u])), uwb[u])
                idx = (run * unroll + u) * CHUNKS_PER_GROUP + c
                b_s[head, idx] = bn[:, :dk]
                n_s[head, idx] = (-bn[:, dk:]).astype(BF16)
                gam_s[head, idx] = jnp.exp(gl[u][c * GDN_CHUNK:c * GDN_CHUNK + 8, :])

    stage(0, stage_sets[0])

    def item_pair(m, carry):
        w = 2 * m
        chains(w, stage_sets[0])
        stage(w + 1, stage_sets[1])
        chains(w + 1, stage_sets[1])
        stage(jnp.minimum(w + 2, n_items - 1), stage_sets[0])
        return carry

    lax.fori_loop(0, n_items // 2, item_pair, 0)

    def scan(c, states):
        rows = pl.ds(pl.multiple_of(c * GDN_CHUNK, GDN_CHUNK), GDN_CHUNK)
        new_states = []
        for head in range(GDN_HEADS_PER_STEP):
            sb = states[head].astype(BF16)
            o_s[head, rows, :] = o_s[head, rows, :] + _dot(qh_s[head, rows, :], sb)
            gam = gam_s[head, c][0:1, :]
            new_states.append(states[head] * gam + (_dot(n_s[head, c], sb) + b_s[head, c]))
        return tuple(new_states)

    state0 = jnp.zeros((dk, dk), F32)
    lax.fori_loop(0, n_chunks, scan, (state0,) * GDN_HEADS_PER_STEP)

    for head in range(GDN_HEADS_PER_STEP):
        o = o_s[head]
        ms = jnp.mean(o * o, axis=-1, keepdims=True)
        o = o * lax.rsqrt(ms + EPS) * nw_ref[...]
        o_ref[0, :, head * LANES:(head + 1) * LANES] = (o * _silu(gate_ref[head, 0].astype(F32))).astype(BF16)


def _gdn(side, ab, conv_w, alog_b, dtb_b, norm_w, batch, seq):
    side4 = side.reshape(SIDE_BLOCKS, batch, seq, LANES)
    ab3 = ab.reshape(batch, seq, AB_WIDTH)
    n_chunks = seq // GDN_CHUNK
    dk = GDN_HEAD_DIM
    hps = GDN_HEADS_PER_STEP
    steps = GDN_HEADS // hps
    run_rows = min(GDN_GROUP_UNROLL, seq // GDN_GROUP) * GDN_GROUP
    assert (hps * (seq // run_rows)) % 2 == 0

    def head_slabs(base):
        return pl.BlockSpec((hps, 1, seq, LANES), lambda b, h: (base // hps + h, b, 0, 0))

    head_vec = pl.BlockSpec((hps, 1, LANES), lambda b, h: (h, 0, 0))
    staging = [pltpu.VMEM((run_rows, LANES), F32)] * 12
    return pl.pallas_call(
        _gdn_kernel,
        out_shape=jax.ShapeDtypeStruct((batch, seq, GDN_WIDTH), BF16),
        grid=(batch, steps),
        in_specs=[
            head_slabs(SIDE_GDN_Q), head_slabs(SIDE_GDN_K), head_slabs(SIDE_GDN_V),
            head_slabs(SIDE_GDN_GATE),
            pl.BlockSpec((1, seq, AB_WIDTH), lambda b, h: (b, 0, 0)),
            pl.BlockSpec((3, hps, CONV_K, LANES), lambda b, h: (0, h, 0, 0)),
            head_vec, head_vec,
            pl.BlockSpec((1, LANES), lambda b, h: (0, 0)),
        ],
        out_specs=pl.BlockSpec((1, seq, hps * LANES), lambda b, h: (b, 0, h)),
        scratch_shapes=staging + [
            pltpu.VMEM((3, 8 + GDN_GROUP, LANES), F32),
            pltpu.VMEM((hps, n_chunks, dk, dk), BF16),
            pltpu.VMEM((hps, n_chunks, dk, dk), F32),
            pltpu.VMEM((hps, seq, dk), BF16),
            pltpu.VMEM((hps, n_chunks, 8, LANES), F32),
            pltpu.VMEM((hps, seq, dk), F32),
        ],
        compiler_params=pltpu.CompilerParams(
            dimension_semantics=("arbitrary", "arbitrary"), vmem_limit_bytes=VMEM_LIMIT),
        name="gdn",
    )(side4, side4, side4, side4, ab3, conv_w, alog_b, dtb_b, norm_w)


def _out_proj_kernel(x_ref, osb_ref, og_ref, msb_ref, mg_ref, wsb_ref, wg_ref, wo_ref, pw_ref, o_ref):
    ya = _dot(osb_ref[...], wsb_ref[...])
    yb = _dot(og_ref[...], wg_ref[...])
    merged = []
    for k in range(D_MODEL // LANES):
        sl = slice(k * LANES, (k + 1) * LANES)
        merged.append((_sigmoid(msb_ref[k].astype(F32)) * ya[:, sl]
                       + _sigmoid(mg_ref[k].astype(F32)) * yb[:, sl]).astype(BF16))
    y = _dot(jnp.concatenate(merged, axis=1), wo_ref[...])
    ms = jnp.mean(y * y, axis=-1, keepdims=True)
    o_ref[...] = x_ref[...] + y * lax.rsqrt(ms + EPS) * pw_ref[...]


def _out_proj(x2d, o_sb, o_g, side, w_sb, w_g, w_o, post_w):
    n = x2d.shape[0]
    tm = ROW_TILE
    const = lambda i: (0, 0)
    merge_blocks = D_MODEL // LANES
    return pl.pallas_call(
        _out_proj_kernel,
        out_shape=jax.ShapeDtypeStruct((n, D_MODEL), F32),
        grid=(n // tm,),
        in_specs=[
            pl.BlockSpec((tm, D_MODEL), lambda i: (i, 0)),
            pl.BlockSpec((tm, SB_WIDTH), lambda i: (i, 0)),
            pl.BlockSpec((tm, GDN_WIDTH), lambda i: (i, 0)),
            pl.BlockSpec((merge_blocks, tm, LANES), lambda i: (SIDE_MERGE_SB // merge_blocks, i, 0)),
            pl.BlockSpec((merge_blocks, tm, LANES), lambda i: (SIDE_MERGE_GDN // merge_blocks, i, 0)),
            pl.BlockSpec((SB_WIDTH, D_MODEL), const),
            pl.BlockSpec((GDN_WIDTH, D_MODEL), const),
            pl.BlockSpec((D_MODEL, D_MODEL), const),
            pl.BlockSpec((1, D_MODEL), const),
        ],
        out_specs=pl.BlockSpec((tm, D_MODEL), lambda i: (i, 0)),
        compiler_params=pltpu.CompilerParams(
            dimension_semantics=("arbitrary",), vmem_limit_bytes=VMEM_LIMIT),
        name="out_proj",
    )(x2d, o_sb, o_g, side, side, w_sb, w_g, w_o, post_w)


def _regroup_w_in(w):
    o_sb_gate = QKV_WIDTH
    o_gdn_qkv = o_sb_gate + SB_WIDTH
    o_gdn_gate = o_gdn_qkv + 3 * GDN_WIDTH
    o_a = o_gdn_gate + GDN_WIDTH
    o_b = o_a + GDN_HEADS
    o_merge = o_b + GDN_HEADS
    wa = jnp.concatenate(
        [w[..., :SB_WIDTH] * (SB_HEAD_DIM ** -0.5 * LOG2_E), w[..., SB_WIDTH:QKV_WIDTH]], axis=-1)
    wb = jnp.concatenate([w[..., o_merge:], w[..., o_sb_gate:o_a]], axis=-1)
    pad = jnp.zeros(w.shape[:-1] + (AB_WIDTH - 2 * GDN_HEADS,), w.dtype)
    wab = jnp.concatenate([w[..., o_a:o_merge], pad], axis=-1)
    return wa.astype(BF16), wb.astype(BF16), wab.astype(BF16)


def kernel(x, pre_norm_w, post_norm_w, w_in, conv_w, a_log, dt_bias, gdn_norm_w,
           w_branch_sb, w_branch_gdn, w_out):
    batch, seq, d = x.shape
    depth = w_in.shape[0]
    assert d == D_MODEL and seq % GDN_GROUP == 0 and seq % SB_TILE == 0 and (batch * seq) % ROW_TILE == 0
    wa, wb, wab = _regroup_w_in(w_in)
    cw = conv_w.astype(F32).reshape(depth, CONV_K, 3, GDN_HEADS, LANES).transpose(0, 2, 3, 1, 4)
    alog_b = jnp.broadcast_to(a_log.astype(F32)[:, :, None, None], (depth, GDN_HEADS, 1, LANES))
    dtb_b = jnp.broadcast_to(dt_bias.astype(F32)[:, :, None, None], (depth, GDN_HEADS, 1, LANES))
    w_sb = w_branch_sb.astype(BF16)
    w_g = w_branch_gdn.astype(BF16)
    w_o = w_out.astype(BF16)
    x2d = x.reshape(batch * seq, d)
    for layer in range(depth):
        qkv, side, ab = _in_proj(x2d, pre_norm_w[layer][None, :], wa[layer], wb[layer], wab[layer])
        o_sb = _sb_attention(qkv, side, batch, seq)
        o_g = _gdn(side, ab, cw[layer], alog_b[layer], dtb_b[layer], gdn_norm_w[layer][None, :], batch, seq)
        x2d = _out_proj(x2d, o_sb.reshape(batch * seq, SB_WIDTH), o_g.reshape(batch * seq, GDN_WIDTH),
                        side, w_sb[layer], w_g[layer], w_o[layer], post_norm_w[layer][None, :])
    return x2d.reshape(batch, seq, d)
```

```python
import functools

import jax
import jax.numpy as jnp
from jax import lax
from jax.experimental import pallas as pl
from jax.experimental.pallas import tpu as pltpu

F32 = jnp.float32
BF16 = jnp.bfloat16

D_MODEL = 1024
SB_HEADS = 8
SB_HEAD_DIM = 64
SB_WIDTH = SB_HEADS * SB_HEAD_DIM
GDN_HEADS = 4
GDN_HEAD_DIM = 128
GDN_WIDTH = GDN_HEADS * GDN_HEAD_DIM
GDN_CHUNK = 64
CONV_K = 4
EPS = 1e-6

LANES = 128
BF16_ROWS = 16
MXU_EDGE = 256
ROW_TILE = 512
SB_TILE = MXU_EDGE
SB_PAIRS_PER_STEP = 2
GDN_GROUP = MXU_EDGE
CHUNKS_PER_GROUP = GDN_GROUP // GDN_CHUNK
GDN_HEADS_PER_STEP = 4
GDN_GROUP_UNROLL = 4
GDN_BASE_BLOCK = 8

SIDE_MERGE_SB = 0
SIDE_MERGE_GDN = 8
SIDE_SB_GATE = 16
SIDE_GDN_Q = 20
SIDE_GDN_K = 24
SIDE_GDN_V = 28
SIDE_GDN_GATE = 32
SIDE_BLOCKS = 36
SIDE_WIDTH = SIDE_BLOCKS * LANES
AB_WIDTH = LANES
QKV_WIDTH = 3 * SB_WIDTH
COL_CHUNK = 512

VMEM_LIMIT = 60 * 1024 * 1024

LOG2_E = 1.4426950408889634
MASKED_LOG = -1e30


def _dot(a, b, dims=None):
    if dims is None:
        return jnp.dot(a, b, preferred_element_type=F32)
    return lax.dot_general(a, b, (dims, ((), ())), preferred_element_type=F32)


def _sigmoid(x):
    return 1.0 / (1.0 + jnp.exp(-x))


def _silu(x):
    return x * _sigmoid(x)


def _softplus(x):
    return jnp.maximum(x, 0.0) + jnp.log(1.0 + jnp.exp(-jnp.abs(x)))


def _in_proj_kernel(x_ref, pw_ref, wa_ref, wb_ref, wab_ref, qkv_ref, side_ref, ab_ref):
    x = x_ref[...]
    ms = jnp.mean(x * x, axis=-1, keepdims=True)
    h = (x * lax.rsqrt(ms + EPS) * pw_ref[...]).astype(BF16)
    per_chunk = COL_CHUNK // LANES
    for c in range(QKV_WIDTH // COL_CHUNK):
        sl = slice(c * COL_CHUNK, (c + 1) * COL_CHUNK)
        qkv_ref[:, sl] = _dot(h, wa_ref[:, sl]).astype(BF16)
    for c in range(SIDE_WIDTH // COL_CHUNK):
        res = _dot(h, wb_ref[:, c * COL_CHUNK:(c + 1) * COL_CHUNK]).astype(BF16)
        for k in range(per_chunk):
            side_ref[c * per_chunk + k] = res[:, k * LANES:(k + 1) * LANES]
    ab_ref[...] = _dot(h, wab_ref[...])


def _in_proj(x2d, pre_w, wa, wb, wab):
    n = x2d.shape[0]
    tm = ROW_TILE
    const = lambda i: (0, 0)
    resident = functools.partial(pl.BlockSpec, index_map=const, pipeline_mode=pl.Buffered(1))
    return pl.pallas_call(
        _in_proj_kernel,
        out_shape=(jax.ShapeDtypeStruct((n, QKV_WIDTH), BF16),
                   jax.ShapeDtypeStruct((SIDE_BLOCKS, n, LANES), BF16),
                   jax.ShapeDtypeStruct((n, AB_WIDTH), F32)),
        grid=(n // tm,),
        in_specs=[
            pl.BlockSpec((tm, D_MODEL), lambda i: (i, 0)),
            pl.BlockSpec((1, D_MODEL), const),
            resident((D_MODEL, QKV_WIDTH)),
            resident((D_MODEL, SIDE_WIDTH)),
            resident((D_MODEL, AB_WIDTH)),
        ],
        out_specs=(pl.BlockSpec((tm, QKV_WIDTH), lambda i: (i, 0)),
                   pl.BlockSpec((SIDE_BLOCKS, tm, LANES), lambda i: (0, i, 0)),
                   pl.BlockSpec((tm, AB_WIDTH), lambda i: (i, 0))),
        compiler_params=pltpu.CompilerParams(
            dimension_semantics=("arbitrary",), vmem_limit_bytes=VMEM_LIMIT),
        name="in_proj",
    )(x2d, pre_w, wa, wb, wab)


def _sb_kernel(q_ref, k_ref, v_ref, gate_ref, o_ref, e_s, acc_s):
    lower_tile = 2 * pl.program_id(2)
    tile = SB_TILE
    n_heads = 2 * SB_PAIRS_PER_STEP
    lower = tuple(range(n_heads))
    upper = tuple(range(n_heads, 2 * n_heads))
    lane = lax.broadcasted_iota(jnp.int32, (tile, LANES), 1)
    row = lax.broadcasted_iota(jnp.int32, (tile, tile), 0)
    col = lax.broadcasted_iota(jnp.int32, (tile, tile), 1)
    first = lane < SB_HEAD_DIM
    causal = col < row
    suffix_rhs = (row > col).astype(BF16)

    def pair_lanes(s):
        pair = (s % n_heads) // 2
        return slice(pair * LANES, (pair + 1) * LANES)

    def query_rows(s):
        return slice((s // n_heads) * tile, (s // n_heads + 1) * tile)

    q_streams = []
    for s in lower + upper:
        q = q_ref[0, query_rows(s), pair_lanes(s)]
        keep = first if s % 2 == 0 else jnp.logical_not(first)
        q_streams.append(jnp.where(keep, q, jnp.zeros_like(q)))

    def key_rows(j):
        return pl.ds(pl.multiple_of(j * tile, tile), tile)

    def stage_a(j, streams, carries, diag_streams):
        z = {s: _dot(q_streams[s], k_ref[0, key_rows(j), pair_lanes(s)], ((1,), (1,))) for s in streams}
        t = {s: jnp.log(1.0 + jnp.exp2(-jnp.abs(z[s]))) * LOG2_E for s in streams}
        log_sig = {s: jnp.minimum(z[s], 0.0) - t[s] for s in streams}
        log_keep = {s: log_sig[s] - z[s] for s in streams}
        for s in diag_streams:
            log_keep[s] = jnp.where(causal, log_keep[s], 0.0)
        suffix = {s: _dot(log_keep[s].astype(BF16), suffix_rhs) for s in streams}
        for s in streams:
            e = (log_sig[s] + suffix[s]) + carries[s]
            if s in diag_streams:
                e = jnp.where(causal, e, MASKED_LOG)
            e_s[s] = e
        new = dict(carries)
        for s in streams:
            new[s] = carries[s] + (suffix[s][:, 0:1] + log_keep[s][:, 0:1])
        return new

    def stage_b(j, streams):
        a = {s: jnp.exp2(e_s[s]).astype(BF16) for s in streams}
        for s in streams:
            acc_s[s] += _dot(a[s], v_ref[0, key_rows(j), pair_lanes(s)])

    acc_s[...] = jnp.zeros_like(acc_s)
    carry0 = jnp.zeros((tile, 1), F32)
    carries = {s: carry0 for s in lower + upper}
    carries = stage_a(lower_tile + 1, upper, carries, upper)
    stage_b(lower_tile + 1, upper)
    carries = stage_a(lower_tile, lower + upper, carries, lower)

    def body(jj, carry_tuple):
        j = lower_tile - jj
        stage_b(j, lower + upper)
        new = stage_a(j - 1, lower + upper, dict(zip(lower + upper, carry_tuple)), ())
        return tuple(new[s] for s in lower + upper)

    lax.fori_loop(0, lower_tile, body, tuple(carries[s] for s in lower + upper))
    stage_b(0, lower + upper)
    for streams in (lower, upper):
        rows = query_rows(streams[0])
        o = jnp.concatenate(
            [jnp.where(first, acc_s[streams[2 * p]], acc_s[streams[2 * p + 1]])
             for p in range(SB_PAIRS_PER_STEP)], axis=1)
        gate = jnp.concatenate(
            [gate_ref[p, 0, rows, :] for p in range(SB_PAIRS_PER_STEP)], axis=1).astype(F32)
        o_ref[0, rows, :] = (o * _silu(gate)).astype(BF16)


def _sb_attention(qkv, side, batch, seq):
    width = SB_PAIRS_PER_STEP * LANES
    steps = SB_WIDTH // width
    qkv3 = qkv.reshape(batch, seq, QKV_WIDTH)
    side4 = side.reshape(SIDE_BLOCKS, batch, seq, LANES)
    gate_base = SIDE_SB_GATE // SB_PAIRS_PER_STEP
    q_rows = 2 * SB_TILE
    n_streams = 2 * 2 * SB_PAIRS_PER_STEP
    return pl.pallas_call(
        _sb_kernel,
        out_shape=jax.ShapeDtypeStruct((batch, seq, SB_WIDTH), BF16),
        grid=(batch, steps, seq // q_rows),
        in_specs=[
            pl.BlockSpec((1, q_rows, width), lambda b, p, i: (b, i, p)),
            pl.BlockSpec((1, seq, width), lambda b, p, i: (b, 0, steps + p)),
            pl.BlockSpec((1, seq, width), lambda b, p, i: (b, 0, 2 * steps + p)),
            pl.BlockSpec((SB_PAIRS_PER_STEP, 1, q_rows, LANES), lambda b, p, i: (gate_base + p, b, i, 0)),
        ],
        out_specs=pl.BlockSpec((1, q_rows, width), lambda b, p, i: (b, i, p)),
        scratch_shapes=[
            pltpu.VMEM((n_streams, SB_TILE, SB_TILE), F32),
            pltpu.VMEM((n_streams, SB_TILE, LANES), F32),
        ],
        compiler_params=pltpu.CompilerParams(
            dimension_semantics=("arbitrary", "arbitrary", "arbitrary"),
            vmem_limit_bytes=VMEM_LIMIT),
        name="sb_attn",
    )(qkv3, qkv3, qkv3, side4)


def _gdn_kernel(q_ref, k_ref, v_ref, gate_ref, ab_ref, cw_ref, alog_ref, dtb_ref, nw_ref, o_ref,
                qa_s, ka_s, va_s, ga_s, gla_s, ba_s, qb_s, kb_s, vb_s, gb_s, glb_s, bb_s,
                conv_s, n_s, b_s, qh_s, gam_s, o_s):
    seq = q_ref.shape[2]
    n_groups = seq // GDN_GROUP
    n_chunks = seq // GDN_CHUNK
    unroll = min(GDN_GROUP_UNROLL, n_groups)
    runs_per_head = n_groups // unroll
    n_items = GDN_HEADS_PER_STEP * runs_per_head
    run_rows = unroll * GDN_GROUP
    dk = GDN_HEAD_DIM
    stage_sets = ((qa_s, ka_s, va_s, ga_s, gla_s, ba_s), (qb_s, kb_s, vb_s, gb_s, glb_s, bb_s))

    in_chunk = lax.broadcasted_iota(jnp.int32, (GDN_GROUP, LANES), 0) % GDN_CHUNK
    ab_lane = lax.broadcasted_iota(jnp.int32, (GDN_GROUP, LANES), 1)
    gi = lax.broadcasted_iota(jnp.int32, (GDN_GROUP, GDN_GROUP), 0)
    gj = lax.broadcasted_iota(jnp.int32, (GDN_GROUP, GDN_GROUP), 1)
    same = (gi // GDN_CHUNK) == (gj // GDN_CHUNK)
    lower = same & (gi >= gj)
    strict = same & (gi > gj)
    eye = (gi == gj).astype(F32)
    base_blocks = (gi // GDN_BASE_BLOCK) == (gj // GDN_BASE_BLOCK)
    joins = []
    s = GDN_BASE_BLOCK
    while s < GDN_CHUNK:
        joins.append(((gi // (2 * s)) == (gj // (2 * s))) & ((gi // s) != (gj // s)))
        s *= 2
    lane_chunk = lax.broadcasted_iota(jnp.int32, (dk, GDN_GROUP), 1) // GDN_CHUNK

    def item_coords(w):
        if runs_per_head == 1:
            return w, 0
        return w // runs_per_head, w % runs_per_head

    def stage(w, refs):
        qn_s, kn_s, vc_s, g_s, gl_s, beta_s = refs
        head, run = item_coords(w)

        def conv_silu(x_ref, section, r0, first_rows):
            cur = x_ref[head, 0, pl.ds(r0, GDN_GROUP), :].astype(F32)
            prev0 = pl.multiple_of(jnp.maximum(r0 - BF16_ROWS, 0), BF16_ROWS)
            prev = x_ref[head, 0, pl.ds(prev0, BF16_ROWS), :].astype(F32)[BF16_ROWS - 8:, :]
            if first_rows:
                prev = jnp.where(r0 > 0, prev, 0.0)
            win = conv_s.at[section]
            win[0:8, :] = prev
            win[8:8 + GDN_GROUP, :] = cur
            w_taps = cw_ref[section, head]
            y = cur * w_taps[CONV_K - 1:CONV_K, :]
            for s in range(1, CONV_K):
                y = y + win[8 - s:8 - s + GDN_GROUP, :] * w_taps[CONV_K - 1 - s:CONV_K - s, :]
            return _silu(y)

        def l2norm(x):
            return x * lax.rsqrt(jnp.sum(x * x, axis=-1, keepdims=True) + EPS)

        for u in range(unroll):
            rw = slice(u * GDN_GROUP, (u + 1) * GDN_GROUP)
            r0 = pl.multiple_of(run * run_rows + u * GDN_GROUP, GDN_GROUP)
            qn_s[rw, :] = l2norm(conv_silu(q_ref, 0, r0, u == 0)) * (dk ** -0.5)
            kn_s[rw, :] = l2norm(conv_silu(k_ref, 1, r0, u == 0))
            vc_s[rw, :] = conv_silu(v_ref, 2, r0, u == 0)
            ab = ab_ref[0, pl.ds(r0, GDN_GROUP), :]
            a_lane = pl.program_id(1) * GDN_HEADS_PER_STEP + head
            a_col = jnp.sum(jnp.where(ab_lane == a_lane, ab, 0.0), axis=-1, keepdims=True)
            b_col = jnp.sum(jnp.where(ab_lane == a_lane + GDN_HEADS, ab, 0.0), axis=-1, keepdims=True)
            a_b = jnp.broadcast_to(a_col, (GDN_GROUP, LANES))
            g = -jnp.exp(alog_ref[head]) * _softplus(a_b + dtb_ref[head])
            s = 1
            while s < GDN_CHUNK:
                g = g + jnp.where(in_chunk >= s, pltpu.roll(g, s, 0), 0.0)
                s *= 2
            g_s[rw, :] = g
            g3 = g.reshape(CHUNKS_PER_GROUP, GDN_CHUNK, LANES)
            gl_s[rw, :] = jnp.broadcast_to(g3[:, GDN_CHUNK - 1:GDN_CHUNK, :], g3.shape).reshape(GDN_GROUP, LANES)
            beta_s[rw, :] = jnp.broadcast_to(_sigmoid(b_col), (GDN_GROUP, LANES))

    def chains(w, refs):
        qn_s, kn_s, vc_s, g_s, gl_s, beta_s = refs
        head, run = item_coords(w)
        each = range(unroll)
        rows = [slice(u * GDN_GROUP, (u + 1) * GDN_GROUP) for u in each]
        kg = [kn_s[rw, :] for rw in rows]
        gc = [g_s[rw, :] for rw in rows]
        bc = [beta_s[rw, :] for rw in rows]
        kgb = [k.astype(BF16) for k in kg]
        kk = [_dot(k, k, ((1,), (1,))) for k in kgb]
        decay = []
        for g in gc:
            g_rowvec = g.T[0:1, :]
            dlog = jnp.concatenate([g, g], axis=1) - g_rowvec
            decay.append(jnp.where(lower, jnp.exp(jnp.where(lower, dlog, 0.0)), 0.0))
        x = [jnp.where(strict, -(jnp.concatenate([bc[u], bc[u]], axis=1) * kk[u] * decay[u]), 0.0)
             for u in each]
        xb = [v.astype(BF16) for v in x]
        y = [jnp.where(base_blocks, v, 0.0) for v in x]
        p = [eye + v for v in y]
        for _ in range(GDN_BASE_BLOCK.bit_length() - 2):
            yb = [v.astype(BF16) for v in y]
            y = [_dot(v, v) for v in yb]
            p = [p[u] + _dot(p[u].astype(BF16), y[u].astype(BF16)) for u in each]
        for join in joins:
            pb = [v.astype(BF16) for v in p]
            t = [_dot(jnp.where(join, xb[u], jnp.zeros_like(xb[u])), pb[u]) for u in each]
            p = [p[u] + _dot(pb[u], t[u].astype(BF16)) for u in each]
        vg = [vc_s[rw, :] for rw in rows]
        rhs = [jnp.concatenate([vg[u] * bc[u], kg[u] * (bc[u] * jnp.exp(gc[u]))], axis=1) for u in each]
        uw = [_dot(p[u].astype(BF16), rhs[u].astype(BF16)) for u in each]
        uwb = [v.astype(BF16) for v in uw]
        qg = [qn_s[rw, :] for rw in rows]
        qk = [_dot(qg[u].astype(BF16), kgb[u], ((1,), (1,))) for u in each]
        intra = [jnp.where(lower, qk[u] * decay[u], 0.0).astype(BF16) for u in each]
        iuw = [_dot(intra[u], uwb[u]) for u in each]
        gl = [gl_s[rw, :] for rw in rows]
        k_til_t = [(kg[u] * jnp.exp(gl[u] - gc[u])).T.astype(BF16) for u in each]
        for u in each:
            out_rows = pl.ds(pl.multiple_of(run * run_rows + u * GDN_GROUP, GDN_GROUP), GDN_GROUP)
            o_s[head, out_rows, :] = iuw[u][:, :dk]
            qh_s[head, out_rows, :] = (qg[u] * jnp.exp(gc[u]) - iuw[u][:, dk:]).astype(BF16)
        for c in range(CHUNKS_PER_GROUP):
            for u in each:
                bn = _dot(jnp.where(lane_chunk == c, k_til_t[u], jnp.zeros_like(k_til_t[u])), uwb[u])
                idx = (run * unroll + u) * CHUNKS_PER_GROUP + c
                b_s[head, idx] = bn[:, :dk]
                n_s[head, idx] = (-bn[:, dk:]).astype(BF16)
                gam_s[head, idx] = jnp.exp(gl[u][c * GDN_CHUNK:c * GDN_CHUNK + 8, :])

    stage(0, stage_sets[0])

    def item_pair(m, carry):
        w = 2 * m
        chains(w, stage_sets[0])
        stage(w + 1, stage_sets[1])
        chains(w + 1, stage_sets[1])
        stage(jnp.minimum(w + 2, n_items - 1), stage_sets[0])
        return carry

    lax.fori_loop(0, n_items // 2, item_pair, 0)

    def scan(c, states):
        rows = pl.ds(pl.multiple_of(c * GDN_CHUNK, GDN_CHUNK), GDN_CHUNK)
        new_states = []
        for head in range(GDN_HEADS_PER_STEP):
            sb = states[head].astype(BF16)
            o_s[head, rows, :] = o_s[head, rows, :] + _dot(qh_s[head, rows, :], sb)
            gam = gam_s[head, c][0:1, :]
            new_states.append(states[head] * gam + (_dot(n_s[head, c], sb) + b_s[head, c]))
        return tuple(new_states)

    state0 = jnp.zeros((dk, dk), F32)
    lax.fori_loop(0, n_chunks, scan, (state0,) * GDN_HEADS_PER_STEP)

    for head in range(GDN_HEADS_PER_STEP):
        o = o_s[head]
        ms = jnp.mean(o * o, axis=-1, keepdims=True)
        o = o * lax.rsqrt(ms + EPS) * nw_ref[...]
        o_ref[0, :, head * LANES:(head + 1) * LANES] = (o * _silu(gate_ref[head, 0].astype(F32))).astype(BF16)


def _gdn(side, ab, conv_w, alog_b, dtb_b, norm_w, batch, seq):
    side4 = side.reshape(SIDE_BLOCKS, batch, seq, LANES)
    ab3 = ab.reshape(batch, seq, AB_WIDTH)
    n_chunks = seq // GDN_CHUNK
    dk = GDN_HEAD_DIM
    hps = GDN_HEADS_PER_STEP
    steps = GDN_HEADS // hps
    run_rows = min(GDN_GROUP_UNROLL, seq // GDN_GROUP) * GDN_GROUP
    assert (hps * (seq // run_rows)) % 2 == 0

    def head_slabs(base):
        return pl.BlockSpec((hps, 1, seq, LANES), lambda b, h: (base // hps + h, b, 0, 0))

    head_vec = pl.BlockSpec((hps, 1, LANES), lambda b, h: (h, 0, 0))
    staging = [pltpu.VMEM((run_rows, LANES), F32)] * 12
    return pl.pallas_call(
        _gdn_kernel,
        out_shape=jax.ShapeDtypeStruct((batch, seq, GDN_WIDTH), BF16),
        grid=(batch, steps),
        in_specs=[
            head_slabs(SIDE_GDN_Q), head_slabs(SIDE_GDN_K), head_slabs(SIDE_GDN_V),
            head_slabs(SIDE_GDN_GATE),
            pl.BlockSpec((1, seq, AB_WIDTH), lambda b, h: (b, 0, 0)),
            pl.BlockSpec((3, hps, CONV_K, LANES), lambda b, h: (0, h, 0, 0)),
            head_vec, head_vec,
            pl.BlockSpec((1, LANES), lambda b, h: (0, 0)),
        ],
        out_specs=pl.BlockSpec((1, seq, hps * LANES), lambda b, h: (b, 0, h)),
        scratch_shapes=staging + [
            pltpu.VMEM((3, 8 + GDN_GROUP, LANES), F32),
            pltpu.VMEM((hps, n_chunks, dk, dk), BF16),
            pltpu.VMEM((hps, n_chunks, dk, dk), F32),
            pltpu.VMEM((hps, seq, dk), BF16),
            pltpu.VMEM((hps, n_chunks, 8, LANES), F32),
            pltpu.VMEM((hps, seq, dk), F32),
        ],
        compiler_params=pltpu.CompilerParams(
            dimension_semantics=("arbitrary", "arbitrary"), vmem_limit_bytes=VMEM_LIMIT),
        name="gdn",
    )(side4, side4, side4, side4, ab3, conv_w, alog_b, dtb_b, norm_w)


def _out_proj_kernel(x_ref, osb_ref, og_ref, msb_ref, mg_ref, wsb_ref, wg_ref, wo_ref, pw_ref, o_ref):
    ya = _dot(osb_ref[...], wsb_ref[...])
    yb = _dot(og_ref[...], wg_ref[...])
    merged = []
    for k in range(D_MODEL // LANES):
        sl = slice(k * LANES, (k + 1) * LANES)
        merged.append((_sigmoid(msb_ref[k].astype(F32)) * ya[:, sl]
                       + _sigmoid(mg_ref[k].astype(F32)) * yb[:, sl]).astype(BF16))
    y = _dot(jnp.concatenate(merged, axis=1), wo_ref[...])
    ms = jnp.mean(y * y, axis=-1, keepdims=True)
    o_ref[...] = x_ref[...] + y * lax.rsqrt(ms + EPS) * pw_ref[...]


def _out_proj(x2d, o_sb, o_g, side, w_sb, w_g, w_o, post_w):
    n = x2d.shape[0]
    tm = ROW_TILE
    const = lambda i: (0, 0)
    merge_blocks = D_MODEL // LANES
    return pl.pallas_call(
        _out_proj_kernel,
        out_shape=jax.ShapeDtypeStruct((n, D_MODEL), F32),
        grid=(n // tm,),
        in_specs=[
            pl.BlockSpec((tm, D_MODEL), lambda i: (i, 0)),
            pl.BlockSpec((tm, SB_WIDTH), lambda i: (i, 0)),
            pl.BlockSpec((tm, GDN_WIDTH), lambda i: (i, 0)),
            pl.BlockSpec((merge_blocks, tm, LANES), lambda i: (SIDE_MERGE_SB // merge_blocks, i, 0)),
            pl.BlockSpec((merge_blocks, tm, LANES), lambda i: (SIDE_MERGE_GDN // merge_blocks, i, 0)),
            pl.BlockSpec((SB_WIDTH, D_MODEL), const),
            pl.BlockSpec((GDN_WIDTH, D_MODEL), const),
            pl.BlockSpec((D_MODEL, D_MODEL), const),
            pl.BlockSpec((1, D_MODEL), const),
        ],
        out_specs=pl.BlockSpec((tm, D_MODEL), lambda i: (i, 0)),
        compiler_params=pltpu.CompilerParams(
            dimension_semantics=("arbitrary",), vmem_limit_bytes=VMEM_LIMIT),
        name="out_proj",
    )(x2d, o_sb, o_g, side, side, w_sb, w_g, w_o, post_w)


def _regroup_w_in(w):
    o_sb_gate = QKV_WIDTH
    o_gdn_qkv = o_sb_gate + SB_WIDTH
    o_gdn_gate = o_gdn_qkv + 3 * GDN_WIDTH
    o_a = o_gdn_gate + GDN_WIDTH
    o_b = o_a + GDN_HEADS
    o_merge = o_b + GDN_HEADS
    wa = jnp.concatenate(
        [(w[..., :SB_WIDTH] * (SB_HEAD_DIM ** -0.5 * LOG2_E)).astype(BF16),
         w[..., SB_WIDTH:QKV_WIDTH].astype(BF16)], axis=-1)
    wb = jnp.concatenate([w[..., o_merge:].astype(BF16), w[..., o_sb_gate:o_a].astype(BF16)], axis=-1)
    pad = jnp.zeros(w.shape[:-1] + (AB_WIDTH - 2 * GDN_HEADS,), BF16)
    wab = jnp.concatenate([w[..., o_a:o_merge].astype(BF16), pad], axis=-1)
    return wa, wb, wab


def kernel(x, pre_norm_w, post_norm_w, w_in, conv_w, a_log, dt_bias, gdn_norm_w,
           w_branch_sb, w_branch_gdn, w_out):
    batch, seq, d = x.shape
    depth = w_in.shape[0]
    assert d == D_MODEL and seq % GDN_GROUP == 0 and seq % (2 * SB_TILE) == 0 and (batch * seq) % ROW_TILE == 0
    wa, wb, wab = _regroup_w_in(w_in)
    cw = conv_w.astype(F32).reshape(depth, CONV_K, 3, GDN_HEADS, LANES).transpose(0, 2, 3, 1, 4)
    alog_b = jnp.broadcast_to(a_log.astype(F32)[:, :, None, None], (depth, GDN_HEADS, 1, LANES))
    dtb_b = jnp.broadcast_to(dt_bias.astype(F32)[:, :, None, None], (depth, GDN_HEADS, 1, LANES))
    w_sb = w_branch_sb.astype(BF16)
    w_g = w_branch_gdn.astype(BF16)
    w_o = w_out.astype(BF16)
    x2d = x.reshape(batch * seq, d)
    for layer in range(depth):
        qkv, side, ab = _in_proj(x2d, pre_norm_w[layer][None, :], wa[layer], wb[layer], wab[layer])
        o_sb = _sb_attention(qkv, side, batch, seq)
        o_g = _gdn(side, ab, cw[layer], alog_b[layer], dtb_b[layer], gdn_norm_w[layer][None, :], batch, seq)
        x2d = _out_proj(x2d, o_sb.reshape(batch * seq, SB_WIDTH), o_g.reshape(batch * seq, GDN_WIDTH),
                        side, w_sb[layer], w_g[layer], w_o[layer], post_norm_w[layer][None, :])
    return x2d.reshape(batch, seq, d)
```

```python
import functools

import jax
import jax.numpy as jnp
from jax import lax
from jax.experimental import pallas as pl
from jax.experimental.pallas import tpu as pltpu

F32 = jnp.float32
BF16 = jnp.bfloat16

D_MODEL = 1024
SB_HEADS = 8
SB_HEAD_DIM = 64
SB_WIDTH = SB_HEADS * SB_HEAD_DIM
GDN_HEADS = 4
GDN_HEAD_DIM = 128
GDN_WIDTH = GDN_HEADS * GDN_HEAD_DIM
GDN_CHUNK = 64
CONV_K = 4
EPS = 1e-6

LANES = 128
BF16_ROWS = 16
MXU_EDGE = 256
ROW_TILE = 512
SB_TILE = MXU_EDGE
SB_PAIRS_PER_STEP = 2
GDN_GROUP = MXU_EDGE
CHUNKS_PER_GROUP = GDN_GROUP // GDN_CHUNK
GDN_HEADS_PER_STEP = 4
GDN_GROUP_UNROLL = 4
GDN_BASE_BLOCK = 8

SIDE_MERGE_SB = 0
SIDE_MERGE_GDN = 8
SIDE_SB_GATE = 16
SIDE_GDN_Q = 20
SIDE_GDN_K = 24
SIDE_GDN_V = 28
SIDE_GDN_GATE = 32
SIDE_BLOCKS = 36
SIDE_WIDTH = SIDE_BLOCKS * LANES
AB_WIDTH = LANES
QKV_WIDTH = 3 * SB_WIDTH
COL_CHUNK = 512

VMEM_LIMIT = 60 * 1024 * 1024

LOG2_E = 1.4426950408889634
MASKED_LOG = -1e30


def _dot(a, b, dims=None):
    if dims is None:
        return jnp.dot(a, b, preferred_element_type=F32)
    return lax.dot_general(a, b, (dims, ((), ())), preferred_element_type=F32)


def _sigmoid(x):
    return 1.0 / (1.0 + jnp.exp(-x))


def _silu(x):
    return x * _sigmoid(x)


def _softplus(x):
    return jnp.maximum(x, 0.0) + jnp.log(1.0 + jnp.exp(-jnp.abs(x)))


def _in_proj_kernel(x_ref, pw_ref, wa_ref, wb_ref, wab_ref, qkv_ref, side_ref, ab_ref):
    x = x_ref[...]
    ms = jnp.mean(x * x, axis=-1, keepdims=True)
    h = (x * lax.rsqrt(ms + EPS) * pw_ref[...]).astype(BF16)
    per_chunk = COL_CHUNK // LANES
    for c in range(QKV_WIDTH // COL_CHUNK):
        sl = slice(c * COL_CHUNK, (c + 1) * COL_CHUNK)
        qkv_ref[:, sl] = _dot(h, wa_ref[:, sl]).astype(BF16)
    for c in range(SIDE_WIDTH // COL_CHUNK):
        res = _dot(h, wb_ref[:, c * COL_CHUNK:(c + 1) * COL_CHUNK]).astype(BF16)
        for k in range(per_chunk):
            side_ref[c * per_chunk + k] = res[:, k * LANES:(k + 1) * LANES]
    ab_ref[...] = _dot(h, wab_ref[...])


def _in_proj(x2d, pre_w, wa, wb, wab):
    n = x2d.shape[0]
    tm = ROW_TILE
    const = lambda i: (0, 0)
    resident = functools.partial(pl.BlockSpec, index_map=const, pipeline_mode=pl.Buffered(1))
    return pl.pallas_call(
        _in_proj_kernel,
        out_shape=(jax.ShapeDtypeStruct((n, QKV_WIDTH), BF16),
                   jax.ShapeDtypeStruct((SIDE_BLOCKS, n, LANES), BF16),
                   jax.ShapeDtypeStruct((n, AB_WIDTH), F32)),
        grid=(n // tm,),
        in_specs=[
            pl.BlockSpec((tm, D_MODEL), lambda i: (i, 0)),
            pl.BlockSpec((1, D_MODEL), const),
            resident((D_MODEL, QKV_WIDTH)),
            resident((D_MODEL, SIDE_WIDTH)),
            resident((D_MODEL, AB_WIDTH)),
        ],
        out_specs=(pl.BlockSpec((tm, QKV_WIDTH), lambda i: (i, 0)),
                   pl.BlockSpec((SIDE_BLOCKS, tm, LANES), lambda i: (0, i, 0)),
                   pl.BlockSpec((tm, AB_WIDTH), lambda i: (i, 0))),
        compiler_params=pltpu.CompilerParams(
            dimension_semantics=("arbitrary",), vmem_limit_bytes=VMEM_LIMIT),
        name="in_proj",
    )(x2d, pre_w, wa, wb, wab)


def _sb_kernel(q_ref, k_ref, v_ref, gate_ref, o_ref, e_s, acc_s):
    lower_tile = 2 * pl.program_id(2)
    tile = SB_TILE
    n_heads = 2 * SB_PAIRS_PER_STEP
    lower = tuple(range(n_heads))
    upper = tuple(range(n_heads, 2 * n_heads))
    lane = lax.broadcasted_iota(jnp.int32, (tile, LANES), 1)
    row = lax.broadcasted_iota(jnp.int32, (tile, tile), 0)
    col = lax.broadcasted_iota(jnp.int32, (tile, tile), 1)
    first = lane < SB_HEAD_DIM
    causal = col < row
    suffix_rhs = (row > col).astype(BF16)

    def pair_lanes(s):
        pair = (s % n_heads) // 2
        return slice(pair * LANES, (pair + 1) * LANES)

    def query_rows(s):
        return slice((s // n_heads) * tile, (s // n_heads + 1) * tile)

    q_streams = []
    for s in lower + upper:
        q = q_ref[0, query_rows(s), pair_lanes(s)]
        keep = first if s % 2 == 0 else jnp.logical_not(first)
        q_streams.append(jnp.where(keep, q, jnp.zeros_like(q)))

    def key_rows(j):
        return pl.ds(pl.multiple_of(j * tile, tile), tile)

    def stage_a(j, streams, carries, diag_streams):
        z = {s: _dot(q_streams[s], k_ref[0, key_rows(j), pair_lanes(s)], ((1,), (1,))) for s in streams}
        t = {s: jnp.log(1.0 + jnp.exp2(-jnp.abs(z[s]))) * LOG2_E for s in streams}
        log_sig = {s: jnp.minimum(z[s], 0.0) - t[s] for s in streams}
        log_keep = {s: log_sig[s] - z[s] for s in streams}
        for s in diag_streams:
            log_keep[s] = jnp.where(causal, log_keep[s], 0.0)
        suffix = {s: _dot(log_keep[s].astype(BF16), suffix_rhs) for s in streams}
        for s in streams:
            e = (log_sig[s] + suffix[s]) + carries[s]
            if s in diag_streams:
                e = jnp.where(causal, e, MASKED_LOG)
            e_s[s] = e
        new = dict(carries)
        for s in streams:
            new[s] = carries[s] + (suffix[s][:, 0:1] + log_keep[s][:, 0:1])
        return new

    def stage_b(j, streams):
        a = {s: jnp.exp2(e_s[s]).astype(BF16) for s in streams}
        for s in streams:
            acc_s[s] += _dot(a[s], v_ref[0, key_rows(j), pair_lanes(s)])

    acc_s[...] = jnp.zeros_like(acc_s)
    carry0 = jnp.zeros((tile, 1), F32)
    carries = {s: carry0 for s in lower + upper}
    carries = stage_a(lower_tile + 1, upper, carries, upper)
    stage_b(lower_tile + 1, upper)
    carries = stage_a(lower_tile, lower + upper, carries, lower)

    def body(jj, carry_tuple):
        j = lower_tile - jj
        stage_b(j, lower + upper)
        new = stage_a(j - 1, lower + upper, dict(zip(lower + upper, carry_tuple)), ())
        return tuple(new[s] for s in lower + upper)

    lax.fori_loop(0, lower_tile, body, tuple(carries[s] for s in lower + upper))
    stage_b(0, lower + upper)
    for streams in (lower, upper):
        rows = query_rows(streams[0])
        o = jnp.concatenate(
            [jnp.where(first, acc_s[streams[2 * p]], acc_s[streams[2 * p + 1]])
             for p in range(SB_PAIRS_PER_STEP)], axis=1)
        gate = jnp.concatenate(
            [gate_ref[p, 0, rows, :] for p in range(SB_PAIRS_PER_STEP)], axis=1).astype(F32)
        o_ref[0, rows, :] = (o * _silu(gate)).astype(BF16)


def _sb_attention(qkv, side, batch, seq):
    width = SB_PAIRS_PER_STEP * LANES
    steps = SB_WIDTH // width
    qkv3 = qkv.reshape(batch, seq, QKV_WIDTH)
    side4 = side.reshape(SIDE_BLOCKS, batch, seq, LANES)
    gate_base = SIDE_SB_GATE // SB_PAIRS_PER_STEP
    q_rows = 2 * SB_TILE
    n_streams = 2 * 2 * SB_PAIRS_PER_STEP
    return pl.pallas_call(
        _sb_kernel,
        out_shape=jax.ShapeDtypeStruct((batch, seq, SB_WIDTH), BF16),
        grid=(batch, steps, seq // q_rows),
        in_specs=[
            pl.BlockSpec((1, q_rows, width), lambda b, p, i: (b, i, p)),
            pl.BlockSpec((1, seq, width), lambda b, p, i: (b, 0, steps + p)),
            pl.BlockSpec((1, seq, width), lambda b, p, i: (b, 0, 2 * steps + p)),
            pl.BlockSpec((SB_PAIRS_PER_STEP, 1, q_rows, LANES), lambda b, p, i: (gate_base + p, b, i, 0)),
        ],
        out_specs=pl.BlockSpec((1, q_rows, width), lambda b, p, i: (b, i, p)),
        scratch_shapes=[
            pltpu.VMEM((n_streams, SB_TILE, SB_TILE), F32),
            pltpu.VMEM((n_streams, SB_TILE, LANES), F32),
        ],
        compiler_params=pltpu.CompilerParams(
            dimension_semantics=("arbitrary", "arbitrary", "arbitrary"),
            vmem_limit_bytes=VMEM_LIMIT),
        name="sb_attn",
    )(qkv3, qkv3, qkv3, side4)


def _gdn_kernel(q_ref, k_ref, v_ref, gate_ref, ab_ref, cw_ref, alog_ref, dtb_ref, nw_ref, o_ref,
                qa_s, ka_s, va_s, ga_s, gla_s, ba_s, qb_s, kb_s, vb_s, gb_s, glb_s, bb_s,
                conv_s, n_s, b_s, qh_s, gam_s, o_s):
    seq = q_ref.shape[2]
    n_groups = seq // GDN_GROUP
    n_chunks = seq // GDN_CHUNK
    unroll = min(GDN_GROUP_UNROLL, n_groups)
    runs_per_head = n_groups // unroll
    n_items = GDN_HEADS_PER_STEP * runs_per_head
    run_rows = unroll * GDN_GROUP
    dk = GDN_HEAD_DIM
    stage_sets = ((qa_s, ka_s, va_s, ga_s, gla_s, ba_s), (qb_s, kb_s, vb_s, gb_s, glb_s, bb_s))

    in_chunk = lax.broadcasted_iota(jnp.int32, (GDN_GROUP, LANES), 0) % GDN_CHUNK
    ab_lane = lax.broadcasted_iota(jnp.int32, (GDN_GROUP, LANES), 1)
    gi = lax.broadcasted_iota(jnp.int32, (GDN_GROUP, GDN_GROUP), 0)
    gj = lax.broadcasted_iota(jnp.int32, (GDN_GROUP, GDN_GROUP), 1)
    same = (gi // GDN_CHUNK) == (gj // GDN_CHUNK)
    lower = same & (gi >= gj)
    strict = same & (gi > gj)
    eye = (gi == gj).astype(F32)
    base_blocks = (gi // GDN_BASE_BLOCK) == (gj // GDN_BASE_BLOCK)
    joins = []
    s = GDN_BASE_BLOCK
    while s < GDN_CHUNK:
        joins.append(((gi // (2 * s)) == (gj // (2 * s))) & ((gi // s) != (gj // s)))
        s *= 2
    lane_chunk = lax.broadcasted_iota(jnp.int32, (dk, GDN_GROUP), 1) // GDN_CHUNK

    def item_coords(w):
        if runs_per_head == 1:
            return w, 0
        return w // runs_per_head, w % runs_per_head

    def stage(w, refs):
        qn_s, kn_s, vc_s, g_s, gl_s, beta_s = refs
        head, run = item_coords(w)

        def conv_silu(x_ref, section, r0, first_rows):
            cur = x_ref[head, 0, pl.ds(r0, GDN_GROUP), :].astype(F32)
            prev0 = pl.multiple_of(jnp.maximum(r0 - BF16_ROWS, 0), BF16_ROWS)
            prev = x_ref[head, 0, pl.ds(prev0, BF16_ROWS), :].astype(F32)[BF16_ROWS - 8:, :]
            if first_rows:
                prev = jnp.where(r0 > 0, prev, 0.0)
            win = conv_s.at[section]
            win[0:8, :] = prev
            win[8:8 + GDN_GROUP, :] = cur
            w_taps = cw_ref[section, head]
            y = cur * w_taps[CONV_K - 1:CONV_K, :]
            for s in range(1, CONV_K):
                y = y + win[8 - s:8 - s + GDN_GROUP, :] * w_taps[CONV_K - 1 - s:CONV_K - s, :]
            return _silu(y)

        def l2norm(x):
            return x * lax.rsqrt(jnp.sum(x * x, axis=-1, keepdims=True) + EPS)

        for u in range(unroll):
            rw = slice(u * GDN_GROUP, (u + 1) * GDN_GROUP)
            r0 = pl.multiple_of(run * run_rows + u * GDN_GROUP, GDN_GROUP)
            qn_s[rw, :] = l2norm(conv_silu(q_ref, 0, r0, u == 0)) * (dk ** -0.5)
            kn_s[rw, :] = l2norm(conv_silu(k_ref, 1, r0, u == 0))
            vc_s[rw, :] = conv_silu(v_ref, 2, r0, u == 0)
            ab = ab_ref[0, pl.ds(r0, GDN_GROUP), :]
            a_lane = pl.program_id(1) * GDN_HEADS_PER_STEP + head
            a_col = jnp.sum(jnp.where(ab_lane == a_lane, ab, 0.0), axis=-1, keepdims=True)
            b_col = jnp.sum(jnp.where(ab_lane == a_lane + GDN_HEADS, ab, 0.0), axis=-1, keepdims=True)
            a_b = jnp.broadcast_to(a_col, (GDN_GROUP, LANES))
            g = -jnp.exp(alog_ref[head]) * _softplus(a_b + dtb_ref[head])
            s = 1
            while s < GDN_CHUNK:
                g = g + jnp.where(in_chunk >= s, pltpu.roll(g, s, 0), 0.0)
                s *= 2
            g_s[rw, :] = g
            g3 = g.reshape(CHUNKS_PER_GROUP, GDN_CHUNK, LANES)
            gl_s[rw, :] = jnp.broadcast_to(g3[:, GDN_CHUNK - 1:GDN_CHUNK, :], g3.shape).reshape(GDN_GROUP, LANES)
            beta_s[rw, :] = jnp.broadcast_to(_sigmoid(b_col), (GDN_GROUP, LANES))

    def chains(w, refs):
        qn_s, kn_s, vc_s, g_s, gl_s, beta_s = refs
        head, run = item_coords(w)
        each = range(unroll)
        rows = [slice(u * GDN_GROUP, (u + 1) * GDN_GROUP) for u in each]
        kg = [kn_s[rw, :] for rw in rows]
        gc = [g_s[rw, :] for rw in rows]
        bc = [beta_s[rw, :] for rw in rows]
        kgb = [k.astype(BF16) for k in kg]
        kk = [_dot(k, k, ((1,), (1,))) for k in kgb]
        decay = []
        for g in gc:
            g_rowvec = g.T[0:1, :]
            dlog = jnp.concatenate([g, g], axis=1) - g_rowvec
            decay.append(jnp.where(lower, jnp.exp(jnp.where(lower, dlog, 0.0)), 0.0))
        x = [jnp.where(strict, -(jnp.concatenate([bc[u], bc[u]], axis=1) * kk[u] * decay[u]), 0.0)
             for u in each]
        xb = [v.astype(BF16) for v in x]
        y = [jnp.where(base_blocks, v, 0.0) for v in x]
        p = [eye + v for v in y]
        for _ in range(GDN_BASE_BLOCK.bit_length() - 2):
            yb = [v.astype(BF16) for v in y]
            y = [_dot(v, v) for v in yb]
            p = [p[u] + _dot(p[u].astype(BF16), y[u].astype(BF16)) for u in each]
        for join in joins:
            pb = [v.astype(BF16) for v in p]
            t = [_dot(jnp.where(join, xb[u], jnp.zeros_like(xb[u])), pb[u]) for u in each]
            p = [p[u] + _dot(pb[u], t[u].astype(BF16)) for u in each]
        vg = [vc_s[rw, :] for rw in rows]
        rhs = [jnp.concatenate([vg[u] * bc[u], kg[u] * (bc[u] * jnp.exp(gc[u]))], axis=1) for u in each]
        uw = [_dot(p[u].astype(BF16), rhs[u].astype(BF16)) for u in each]
        uwb = [v.astype(BF16) for v in uw]
        qg = [qn_s[rw, :] for rw in rows]
        qk = [_dot(qg[u].astype(BF16), kgb[u], ((1,), (1,))) for u in each]
        intra = [jnp.where(lower, qk[u] * decay[u], 0.0).astype(BF16) for u in each]
        iuw = [_dot(intra[u], uwb[u]) for u in each]
        gl = [gl_s[rw, :] for rw in rows]
        k_til_t = [(kg[u] * jnp.exp(gl[u] - gc[u])).T.astype(BF16) for u in each]
        for u in each:
            out_rows = pl.ds(pl.multiple_of(run * run_rows + u * GDN_GROUP, GDN_GROUP), GDN_GROUP)
            o_s[head, out_rows, :] = iuw[u][:, :dk]
            qh_s[head, out_rows, :] = (qg[u] * jnp.exp(gc[u]) - iuw[u][:, dk:]).astype(BF16)
        for c in range(CHUNKS_PER_GROUP):
            for u in each:
                bn = _dot(jnp.where(lane_chunk == c, k_til_t[u], jnp.zeros_like(k_til_t[u])), uwb[u])
                idx = (run * unroll + u) * CHUNKS_PER_GROUP + c
                b_s[head, idx] = bn[:, :dk]
                n_s[head, idx] = (-bn[:, dk:]).astype(BF16)
                gam_s[head, idx] = jnp.exp(gl[u][c * GDN_CHUNK:c * GDN_CHUNK + 8, :])

    stage(0, stage_sets[0])

    def item_pair(m, carry):
        w = 2 * m
        chains(w, stage_sets[0])
        stage(w + 1, stage_sets[1])
        chains(w + 1, stage_sets[1])
        stage(jnp.minimum(w + 2, n_items - 1), stage_sets[0])
        return carry

    lax.fori_loop(0, n_items // 2, item_pair, 0)

    def scan(c, states):
        rows = pl.ds(pl.multiple_of(c * GDN_CHUNK, GDN_CHUNK), GDN_CHUNK)
        finalize(jnp.maximum(c - 1, 0))
        new_states = []
        for head in range(GDN_HEADS_PER_STEP):
            sb = states[head].astype(BF16)
            o_s[head, rows, :] = o_s[head, rows, :] + _dot(qh_s[head, rows, :], sb)
            gam = gam_s[head, c][0:1, :]
            new_states.append(states[head] * gam + (_dot(n_s[head, c], sb) + b_s[head, c]))
        return tuple(new_states)

    def finalize(c):
        rows = pl.ds(pl.multiple_of(c * GDN_CHUNK, GDN_CHUNK), GDN_CHUNK)
        for head in range(GDN_HEADS_PER_STEP):
            o = o_s[head, rows, :]
            ms = jnp.mean(o * o, axis=-1, keepdims=True)
            o = o * lax.rsqrt(ms + EPS) * nw_ref[...]
            gate = gate_ref[head, 0, rows, :].astype(F32)
            o_ref[0, rows, head * LANES:(head + 1) * LANES] = (o * _silu(gate)).astype(BF16)

    state0 = jnp.zeros((dk, dk), F32)
    lax.fori_loop(0, n_chunks, scan, (state0,) * GDN_HEADS_PER_STEP)
    finalize(n_chunks - 1)


def _gdn(side, ab, conv_w, alog_b, dtb_b, norm_w, batch, seq):
    side4 = side.reshape(SIDE_BLOCKS, batch, seq, LANES)
    ab3 = ab.reshape(batch, seq, AB_WIDTH)
    n_chunks = seq // GDN_CHUNK
    dk = GDN_HEAD_DIM
    hps = GDN_HEADS_PER_STEP
    steps = GDN_HEADS // hps
    run_rows = min(GDN_GROUP_UNROLL, seq // GDN_GROUP) * GDN_GROUP
    assert (hps * (seq // run_rows)) % 2 == 0

    def head_slabs(base):
        return pl.BlockSpec((hps, 1, seq, LANES), lambda b, h: (base // hps + h, b, 0, 0))

    head_vec = pl.BlockSpec((hps, 1, LANES), lambda b, h: (h, 0, 0))
    staging = [pltpu.VMEM((run_rows, LANES), F32)] * 12
    return pl.pallas_call(
        _gdn_kernel,
        out_shape=jax.ShapeDtypeStruct((batch, seq, GDN_WIDTH), BF16),
        grid=(batch, steps),
        in_specs=[
            head_slabs(SIDE_GDN_Q), head_slabs(SIDE_GDN_K), head_slabs(SIDE_GDN_V),
            head_slabs(SIDE_GDN_GATE),
            pl.BlockSpec((1, seq, AB_WIDTH), lambda b, h: (b, 0, 0)),
            pl.BlockSpec((3, hps, CONV_K, LANES), lambda b, h: (0, h, 0, 0)),
            head_vec, head_vec,
            pl.BlockSpec((1, LANES), lambda b, h: (0, 0)),
        ],
        out_specs=pl.BlockSpec((1, seq, hps * LANES), lambda b, h: (b, 0, h)),
        scratch_shapes=staging + [
            pltpu.VMEM((3, 8 + GDN_GROUP, LANES), F32),
            pltpu.VMEM((hps, n_chunks, dk, dk), BF16),
            pltpu.VMEM((hps, n_chunks, dk, dk), F32),
            pltpu.VMEM((hps, seq, dk), BF16),
            pltpu.VMEM((hps, n_chunks, 8, LANES), F32),
            pltpu.VMEM((hps, seq, dk), F32),
        ],
        compiler_params=pltpu.CompilerParams(
            dimension_semantics=("arbitrary", "arbitrary"), vmem_limit_bytes=VMEM_LIMIT),
        name="gdn",
    )(side4, side4, side4, side4, ab3, conv_w, alog_b, dtb_b, norm_w)


def _out_proj_kernel(x_ref, osb_ref, og_ref, msb_ref, mg_ref, wsb_ref, wg_ref, wo_ref, pw_ref, o_ref):
    ya = _dot(osb_ref[...], wsb_ref[...])
    yb = _dot(og_ref[...], wg_ref[...])
    merged = []
    for k in range(D_MODEL // LANES):
        sl = slice(k * LANES, (k + 1) * LANES)
        merged.append((_sigmoid(msb_ref[k].astype(F32)) * ya[:, sl]
                       + _sigmoid(mg_ref[k].astype(F32)) * yb[:, sl]).astype(BF16))
    y = _dot(jnp.concatenate(merged, axis=1), wo_ref[...])
    ms = jnp.mean(y * y, axis=-1, keepdims=True)
    o_ref[...] = x_ref[...] + y * lax.rsqrt(ms + EPS) * pw_ref[...]


def _out_proj(x2d, o_sb, o_g, side, w_sb, w_g, w_o, post_w):
    n = x2d.shape[0]
    tm = ROW_TILE
    const = lambda i: (0, 0)
    merge_blocks = D_MODEL // LANES
    return pl.pallas_call(
        _out_proj_kernel,
        out_shape=jax.ShapeDtypeStruct((n, D_MODEL), F32),
        grid=(n // tm,),
        in_specs=[
            pl.BlockSpec((tm, D_MODEL), lambda i: (i, 0)),
            pl.BlockSpec((tm, SB_WIDTH), lambda i: (i, 0)),
            pl.BlockSpec((tm, GDN_WIDTH), lambda i: (i, 0)),
            pl.BlockSpec((merge_blocks, tm, LANES), lambda i: (SIDE_MERGE_SB // merge_blocks, i, 0)),
            pl.BlockSpec((merge_blocks, tm, LANES), lambda i: (SIDE_MERGE_GDN // merge_blocks, i, 0)),
            pl.BlockSpec((SB_WIDTH, D_MODEL), const),
            pl.BlockSpec((GDN_WIDTH, D_MODEL), const),
            pl.BlockSpec((D_MODEL, D_MODEL), const),
            pl.BlockSpec((1, D_MODEL), const),
        ],
        out_specs=pl.BlockSpec((tm, D_MODEL), lambda i: (i, 0)),
        compiler_params=pltpu.CompilerParams(
            dimension_semantics=("arbitrary",), vmem_limit_bytes=VMEM_LIMIT),
        name="out_proj",
    )(x2d, o_sb, o_g, side, side, w_sb, w_g, w_o, post_w)


W_OFF_SB_GATE = QKV_WIDTH
W_OFF_AB = W_OFF_SB_GATE + SB_WIDTH + 3 * GDN_WIDTH + GDN_WIDTH
W_OFF_MERGE = W_OFF_AB + 2 * GDN_HEADS
W_IN_WIDTH = W_OFF_MERGE + 2 * D_MODEL
W_PREP_ROWS = 128


def _w_prep_kernel(w_ref, wa_ref, wb_ref, wab_ref):
    w = w_ref[0]
    wa_ref[0, :, :SB_WIDTH] = (w[:, :SB_WIDTH] * (SB_HEAD_DIM ** -0.5 * LOG2_E)).astype(BF16)
    wa_ref[0, :, SB_WIDTH:] = w[:, SB_WIDTH:QKV_WIDTH].astype(BF16)
    wb_ref[0, :, :2 * D_MODEL] = w[:, W_OFF_MERGE:].astype(BF16)
    wb_ref[0, :, 2 * D_MODEL:] = w[:, W_OFF_SB_GATE:W_OFF_AB].astype(BF16)
    lane = lax.broadcasted_iota(jnp.int32, (w.shape[0], AB_WIDTH), 1)
    wab_ref[0] = jnp.where(lane < 2 * GDN_HEADS, w[:, W_OFF_AB:W_OFF_AB + AB_WIDTH], 0.0).astype(BF16)


def _regroup_w_in(w):
    depth = w.shape[0]
    assert w.shape[1:] == (D_MODEL, W_IN_WIDTH)
    rows = W_PREP_ROWS

    def out(width):
        return pl.BlockSpec((1, rows, width), lambda l, r: (l, r, 0))

    return pl.pallas_call(
        _w_prep_kernel,
        out_shape=(jax.ShapeDtypeStruct((depth, D_MODEL, QKV_WIDTH), BF16),
                   jax.ShapeDtypeStruct((depth, D_MODEL, SIDE_WIDTH), BF16),
                   jax.ShapeDtypeStruct((depth, D_MODEL, AB_WIDTH), BF16)),
        grid=(depth, D_MODEL // rows),
        in_specs=[pl.BlockSpec((1, rows, W_IN_WIDTH), lambda l, r: (l, r, 0))],
        out_specs=(out(QKV_WIDTH), out(SIDE_WIDTH), out(AB_WIDTH)),
        compiler_params=pltpu.CompilerParams(
            dimension_semantics=("arbitrary", "arbitrary"), vmem_limit_bytes=VMEM_LIMIT),
        name="w_prep",
    )(w)


def kernel(x, pre_norm_w, post_norm_w, w_in, conv_w, a_log, dt_bias, gdn_norm_w,
           w_branch_sb, w_branch_gdn, w_out):
    batch, seq, d = x.shape
    depth = w_in.shape[0]
    assert d == D_MODEL and seq % GDN_GROUP == 0 and seq % (2 * SB_TILE) == 0 and (batch * seq) % ROW_TILE == 0
    wa, wb, wab = _regroup_w_in(w_in)
    cw = conv_w.astype(F32).reshape(depth, CONV_K, 3, GDN_HEADS, LANES).transpose(0, 2, 3, 1, 4)
    alog_b = jnp.broadcast_to(a_log.astype(F32)[:, :, None, None], (depth, GDN_HEADS, 1, LANES))
    dtb_b = jnp.broadcast_to(dt_bias.astype(F32)[:, :, None, None], (depth, GDN_HEADS, 1, LANES))
    w_sb = w_branch_sb.astype(BF16)
    w_g = w_branch_gdn.astype(BF16)
    w_o = w_out.astype(BF16)
    x2d = x.reshape(batch * seq, d)
    for layer in range(depth):
        qkv, side, ab = _in_proj(x2d, pre_norm_w[layer][None, :], wa[layer], wb[layer], wab[layer])
        o_sb = _sb_attention(qkv, side, batch, seq)
        o_g = _gdn(side, ab, cw[layer], alog_b[layer], dtb_b[layer], gdn_norm_w[layer][None, :], batch, seq)
        x2d = _out_proj(x2d, o_sb.reshape(batch * seq, SB_WIDTH), o_g.reshape(batch * seq, GDN_WIDTH),
                        side, w_sb[layer], w_g[layer], w_o[layer], post_norm_w[layer][None, :])
    return x2d.reshape(batch, seq, d)
```

```python
import functools

import jax
import jax.numpy as jnp
from jax import lax
from jax.experimental import pallas as pl
from jax.experimental.pallas import tpu as pltpu

F32 = jnp.float32
BF16 = jnp.bfloat16

D_MODEL = 1024
SB_HEADS = 8
SB_HEAD_DIM = 64
SB_WIDTH = SB_HEADS * SB_HEAD_DIM
GDN_HEADS = 4
GDN_HEAD_DIM = 128
GDN_WIDTH = GDN_HEADS * GDN_HEAD_DIM
GDN_CHUNK = 64
CONV_K = 4
EPS = 1e-6

LANES = 128
BF16_ROWS = 16
MXU_EDGE = 256
ROW_TILE = 512
SB_TILE = MXU_EDGE
SB_PAIRS_PER_STEP = 2
GDN_GROUP = MXU_EDGE
CHUNKS_PER_GROUP = GDN_GROUP // GDN_CHUNK
GDN_HEADS_PER_STEP = 4
GDN_GROUP_UNROLL = 4
GDN_BASE_BLOCK = 8

SIDE_MERGE_SB = 0
SIDE_MERGE_GDN = 8
SIDE_SB_GATE = 16
SIDE_GDN_Q = 20
SIDE_GDN_K = 24
SIDE_GDN_V = 28
SIDE_GDN_GATE = 32
SIDE_BLOCKS = 36
SIDE_WIDTH = SIDE_BLOCKS * LANES
AB_WIDTH = LANES
QKV_WIDTH = 3 * SB_WIDTH
COL_CHUNK = 512

VMEM_LIMIT = 60 * 1024 * 1024

LOG2_E = 1.4426950408889634
MASKED_LOG = -1e30


def _dot(a, b, dims=None):
    if dims is None:
        return jnp.dot(a, b, preferred_element_type=F32)
    return lax.dot_general(a, b, (dims, ((), ())), preferred_element_type=F32)


def _sigmoid(x):
    return 1.0 / (1.0 + jnp.exp(-x))


def _silu(x):
    return x * _sigmoid(x)


def _softplus(x):
    return jnp.maximum(x, 0.0) + jnp.log(1.0 + jnp.exp(-jnp.abs(x)))


def _in_proj_kernel(x_ref, pw_ref, wa_ref, wb_ref, wab_ref, qkv_ref, side_ref, ab_ref):
    x = x_ref[...]
    ms = jnp.mean(x * x, axis=-1, keepdims=True)
    h = (x * lax.rsqrt(ms + EPS) * pw_ref[...]).astype(BF16)
    per_chunk = COL_CHUNK // LANES
    for c in range(QKV_WIDTH // COL_CHUNK):
        sl = slice(c * COL_CHUNK, (c + 1) * COL_CHUNK)
        qkv_ref[:, sl] = _dot(h, wa_ref[:, sl]).astype(BF16)
    for c in range(SIDE_WIDTH // COL_CHUNK):
        res = _dot(h, wb_ref[:, c * COL_CHUNK:(c + 1) * COL_CHUNK]).astype(BF16)
        for k in range(per_chunk):
            side_ref[c * per_chunk + k] = res[:, k * LANES:(k + 1) * LANES]
    ab_ref[...] = _dot(h, wab_ref[...])


def _in_proj(x2d, pre_w, wa, wb, wab):
    n = x2d.shape[0]
    tm = ROW_TILE
    const = lambda i: (0, 0)
    resident = functools.partial(pl.BlockSpec, index_map=const, pipeline_mode=pl.Buffered(1))
    return pl.pallas_call(
        _in_proj_kernel,
        out_shape=(jax.ShapeDtypeStruct((n, QKV_WIDTH), BF16),
                   jax.ShapeDtypeStruct((SIDE_BLOCKS, n, LANES), BF16),
                   jax.ShapeDtypeStruct((n, AB_WIDTH), F32)),
        grid=(n // tm,),
        in_specs=[
            pl.BlockSpec((tm, D_MODEL), lambda i: (i, 0)),
            pl.BlockSpec((1, D_MODEL), const),
            resident((D_MODEL, QKV_WIDTH)),
            resident((D_MODEL, SIDE_WIDTH)),
            resident((D_MODEL, AB_WIDTH)),
        ],
        out_specs=(pl.BlockSpec((tm, QKV_WIDTH), lambda i: (i, 0)),
                   pl.BlockSpec((SIDE_BLOCKS, tm, LANES), lambda i: (0, i, 0)),
                   pl.BlockSpec((tm, AB_WIDTH), lambda i: (i, 0))),
        compiler_params=pltpu.CompilerParams(
            dimension_semantics=("arbitrary",), vmem_limit_bytes=VMEM_LIMIT),
        name="in_proj",
    )(x2d, pre_w, wa, wb, wab)


def _sb_kernel(q_ref, k_ref, v_ref, gate_ref, o_ref, e_s, acc_s):
    lower_tile = 2 * pl.program_id(2)
    tile = SB_TILE
    n_heads = 2 * SB_PAIRS_PER_STEP
    lower = tuple(range(n_heads))
    upper = tuple(range(n_heads, 2 * n_heads))
    lane = lax.broadcasted_iota(jnp.int32, (tile, LANES), 1)
    row = lax.broadcasted_iota(jnp.int32, (tile, tile), 0)
    col = lax.broadcasted_iota(jnp.int32, (tile, tile), 1)
    first = lane < SB_HEAD_DIM
    causal = col < row
    suffix_rhs = (row > col).astype(BF16)

    def pair_lanes(s):
        pair = (s % n_heads) // 2
        return slice(pair * LANES, (pair + 1) * LANES)

    def query_rows(s):
        return slice((s // n_heads) * tile, (s // n_heads + 1) * tile)

    q_streams = []
    for s in lower + upper:
        q = q_ref[0, query_rows(s), pair_lanes(s)]
        keep = first if s % 2 == 0 else jnp.logical_not(first)
        q_streams.append(jnp.where(keep, q, jnp.zeros_like(q)))

    def key_rows(j):
        return pl.ds(pl.multiple_of(j * tile, tile), tile)

    def stage_a(j, streams, carries, diag_streams):
        z = {s: _dot(q_streams[s], k_ref[0, key_rows(j), pair_lanes(s)], ((1,), (1,))) for s in streams}
        t = {s: jnp.log(1.0 + jnp.exp2(-jnp.abs(z[s]))) * LOG2_E for s in streams}
        log_sig = {s: jnp.minimum(z[s], 0.0) - t[s] for s in streams}
        log_keep = {s: log_sig[s] - z[s] for s in streams}
        for s in diag_streams:
            log_keep[s] = jnp.where(causal, log_keep[s], 0.0)
        suffix = {s: _dot(log_keep[s].astype(BF16), suffix_rhs) for s in streams}
        for s in streams:
            e = (log_sig[s] + suffix[s]) + carries[s]
            if s in diag_streams:
                e = jnp.where(causal, e, MASKED_LOG)
            e_s[s] = e
        new = dict(carries)
        for s in streams:
            new[s] = carries[s] + (suffix[s][:, 0:1] + log_keep[s][:, 0:1])
        return new

    def stage_b(j, streams):
        a = {s: jnp.exp2(e_s[s]).astype(BF16) for s in streams}
        for s in streams:
            acc_s[s] += _dot(a[s], v_ref[0, key_rows(j), pair_lanes(s)])

    acc_s[...] = jnp.zeros_like(acc_s)
    carry0 = jnp.zeros((tile, 1), F32)
    carries = {s: carry0 for s in lower + upper}
    carries = stage_a(lower_tile + 1, upper, carries, upper)
    stage_b(lower_tile + 1, upper)
    carries = stage_a(lower_tile, lower + upper, carries, lower)

    def body(jj, carry_tuple):
        j = lower_tile - jj
        stage_b(j, lower + upper)
        new = stage_a(j - 1, lower + upper, dict(zip(lower + upper, carry_tuple)), ())
        return tuple(new[s] for s in lower + upper)

    lax.fori_loop(0, lower_tile, body, tuple(carries[s] for s in lower + upper))
    stage_b(0, lower + upper)
    for streams in (lower, upper):
        rows = query_rows(streams[0])
        o = jnp.concatenate(
            [jnp.where(first, acc_s[streams[2 * p]], acc_s[streams[2 * p + 1]])
             for p in range(SB_PAIRS_PER_STEP)], axis=1)
        gate = jnp.concatenate(
            [gate_ref[p, 0, rows, :] for p in range(SB_PAIRS_PER_STEP)], axis=1).astype(F32)
        o_ref[0, rows, :] = (o * _silu(gate)).astype(BF16)


def _sb_attention(qkv, side, batch, seq):
    width = SB_PAIRS_PER_STEP * LANES
    steps = SB_WIDTH // width
    qkv3 = qkv.reshape(batch, seq, QKV_WIDTH)
    side4 = side.reshape(SIDE_BLOCKS, batch, seq, LANES)
    gate_base = SIDE_SB_GATE // SB_PAIRS_PER_STEP
    q_rows = 2 * SB_TILE
    n_streams = 2 * 2 * SB_PAIRS_PER_STEP
    return pl.pallas_call(
        _sb_kernel,
        out_shape=jax.ShapeDtypeStruct((batch, seq, SB_WIDTH), BF16),
        grid=(batch, steps, seq // q_rows),
        in_specs=[
            pl.BlockSpec((1, q_rows, width), lambda b, p, i: (b, i, p)),
            pl.BlockSpec((1, seq, width), lambda b, p, i: (b, 0, steps + p)),
            pl.BlockSpec((1, seq, width), lambda b, p, i: (b, 0, 2 * steps + p)),
            pl.BlockSpec((SB_PAIRS_PER_STEP, 1, q_rows, LANES), lambda b, p, i: (gate_base + p, b, i, 0)),
        ],
        out_specs=pl.BlockSpec((1, q_rows, width), lambda b, p, i: (b, i, p)),
        scratch_shapes=[
            pltpu.VMEM((n_streams, SB_TILE, SB_TILE), F32),
            pltpu.VMEM((n_streams, SB_TILE, LANES), F32),
        ],
        compiler_params=pltpu.CompilerParams(
            dimension_semantics=("arbitrary", "arbitrary", "arbitrary"),
            vmem_limit_bytes=VMEM_LIMIT),
        name="sb_attn",
    )(qkv3, qkv3, qkv3, side4)


def _gdn_kernel(q_ref, k_ref, v_ref, gate_ref, ab_ref, cw_ref, alog_ref, dtb_ref, nw_ref, o_ref,
                qa_s, ka_s, va_s, ga_s, gla_s, ba_s, qb_s, kb_s, vb_s, gb_s, glb_s, bb_s,
                conv_s, n_s, b_s, qh_s, gam_s, o_s):
    seq = q_ref.shape[2]
    n_groups = seq // GDN_GROUP
    n_chunks = seq // GDN_CHUNK
    unroll = min(GDN_GROUP_UNROLL, n_groups)
    runs_per_head = n_groups // unroll
    n_items = GDN_HEADS_PER_STEP * runs_per_head
    run_rows = unroll * GDN_GROUP
    dk = GDN_HEAD_DIM
    stage_sets = ((qa_s, ka_s, va_s, ga_s, gla_s, ba_s), (qb_s, kb_s, vb_s, gb_s, glb_s, bb_s))

    in_chunk = lax.broadcasted_iota(jnp.int32, (GDN_GROUP, LANES), 0) % GDN_CHUNK
    ab_lane = lax.broadcasted_iota(jnp.int32, (GDN_GROUP, LANES), 1)
    gi = lax.broadcasted_iota(jnp.int32, (GDN_GROUP, GDN_GROUP), 0)
    gj = lax.broadcasted_iota(jnp.int32, (GDN_GROUP, GDN_GROUP), 1)
    same = (gi // GDN_CHUNK) == (gj // GDN_CHUNK)
    lower = same & (gi >= gj)
    strict = same & (gi > gj)
    eye = (gi == gj).astype(F32)
    base_blocks = (gi // GDN_BASE_BLOCK) == (gj // GDN_BASE_BLOCK)
    joins = []
    s = GDN_BASE_BLOCK
    while s < GDN_CHUNK:
        joins.append(((gi // (2 * s)) == (gj // (2 * s))) & ((gi // s) != (gj // s)))
        s *= 2
    lane_chunk = lax.broadcasted_iota(jnp.int32, (dk, GDN_GROUP), 1) // GDN_CHUNK

    def item_coords(w):
        if runs_per_head == 1:
            return w, 0
        return w // runs_per_head, w % runs_per_head

    def stage(w, refs):
        qn_s, kn_s, vc_s, g_s, gl_s, beta_s = refs
        head, run = item_coords(w)

        def conv_silu(x_ref, section, r0, first_rows):
            cur = x_ref[head, 0, pl.ds(r0, GDN_GROUP), :].astype(F32)
            prev0 = pl.multiple_of(jnp.maximum(r0 - BF16_ROWS, 0), BF16_ROWS)
            prev = x_ref[head, 0, pl.ds(prev0, BF16_ROWS), :].astype(F32)[BF16_ROWS - 8:, :]
            if first_rows:
                prev = jnp.where(r0 > 0, prev, 0.0)
            win = conv_s.at[section]
            win[0:8, :] = prev
            win[8:8 + GDN_GROUP, :] = cur
            w_taps = cw_ref[section, head]
            y = cur * w_taps[CONV_K - 1:CONV_K, :]
            for s in range(1, CONV_K):
                y = y + win[8 - s:8 - s + GDN_GROUP, :] * w_taps[CONV_K - 1 - s:CONV_K - s, :]
            return _silu(y)

        def l2norm(x):
            return x * lax.rsqrt(jnp.sum(x * x, axis=-1, keepdims=True) + EPS)

        for u in range(unroll):
            rw = slice(u * GDN_GROUP, (u + 1) * GDN_GROUP)
            r0 = pl.multiple_of(run * run_rows + u * GDN_GROUP, GDN_GROUP)
            qn_s[rw, :] = l2norm(conv_silu(q_ref, 0, r0, u == 0)) * (dk ** -0.5)
            kn_s[rw, :] = l2norm(conv_silu(k_ref, 1, r0, u == 0))
            vc_s[rw, :] = conv_silu(v_ref, 2, r0, u == 0)
            ab = ab_ref[0, pl.ds(r0, GDN_GROUP), :]
            a_lane = pl.program_id(1) * GDN_HEADS_PER_STEP + head
            a_col = jnp.sum(jnp.where(ab_lane == a_lane, ab, 0.0), axis=-1, keepdims=True)
            b_col = jnp.sum(jnp.where(ab_lane == a_lane + GDN_HEADS, ab, 0.0), axis=-1, keepdims=True)
            a_b = jnp.broadcast_to(a_col, (GDN_GROUP, LANES))
            g = -jnp.exp(alog_ref[head]) * _softplus(a_b + dtb_ref[head])
            s = 1
            while s < GDN_CHUNK:
                g = g + jnp.where(in_chunk >= s, pltpu.roll(g, s, 0), 0.0)
                s *= 2
            g_s[rw, :] = g
            g3 = g.reshape(CHUNKS_PER_GROUP, GDN_CHUNK, LANES)
            gl_s[rw, :] = jnp.broadcast_to(g3[:, GDN_CHUNK - 1:GDN_CHUNK, :], g3.shape).reshape(GDN_GROUP, LANES)
            beta_s[rw, :] = jnp.broadcast_to(_sigmoid(b_col), (GDN_GROUP, LANES))

    def chains(w, refs):
        qn_s, kn_s, vc_s, g_s, gl_s, beta_s = refs
        head, run = item_coords(w)
        each = range(unroll)
        rows = [slice(u * GDN_GROUP, (u + 1) * GDN_GROUP) for u in each]
        kg = [kn_s[rw, :] for rw in rows]
        gc = [g_s[rw, :] for rw in rows]
        bc = [beta_s[rw, :] for rw in rows]
        kgb = [k.astype(BF16) for k in kg]
        kk = [_dot(k, k, ((1,), (1,))) for k in kgb]
        decay = []
        for g in gc:
            g_rowvec = g.T[0:1, :]
            dlog = jnp.concatenate([g, g], axis=1) - g_rowvec
            decay.append(jnp.where(lower, jnp.exp(jnp.where(lower, dlog, 0.0)), 0.0))
        x = [jnp.where(strict, -(jnp.concatenate([bc[u], bc[u]], axis=1) * kk[u] * decay[u]), 0.0)
             for u in each]
        xb = [v.astype(BF16) for v in x]
        y = [jnp.where(base_blocks, v, 0.0) for v in x]
        p = [eye + v for v in y]
        for _ in range(GDN_BASE_BLOCK.bit_length() - 2):
            yb = [v.astype(BF16) for v in y]
            y = [_dot(v, v) for v in yb]
            p = [p[u] + _dot(p[u].astype(BF16), y[u].astype(BF16)) for u in each]
        for join in joins:
            pb = [v.astype(BF16) for v in p]
            t = [_dot(jnp.where(join, xb[u], jnp.zeros_like(xb[u])), pb[u]) for u in each]
            p = [p[u] + _dot(pb[u], t[u].astype(BF16)) for u in each]
        vg = [vc_s[rw, :] for rw in rows]
        rhs = [jnp.concatenate([vg[u] * bc[u], kg[u] * (bc[u] * jnp.exp(gc[u]))], axis=1) for u in each]
        uw = [_dot(p[u].astype(BF16), rhs[u].astype(BF16)) for u in each]
        uwb = [v.astype(BF16) for v in uw]
        qg = [qn_s[rw, :] for rw in rows]
        qk = [_dot(qg[u].astype(BF16), kgb[u], ((1,), (1,))) for u in each]
        intra = [jnp.where(lower, qk[u] * decay[u], 0.0).astype(BF16) for u in each]
        iuw = [_dot(intra[u], uwb[u]) for u in each]
        gl = [gl_s[rw, :] for rw in rows]
        k_til_t = [(kg[u] * jnp.exp(gl[u] - gc[u])).T.astype(BF16) for u in each]
        for u in each:
            out_rows = pl.ds(pl.multiple_of(run * run_rows + u * GDN_GROUP, GDN_GROUP), GDN_GROUP)
            o_s[head, out_rows, :] = iuw[u][:, :dk]
            qh_s[head, out_rows, :] = (qg[u] * jnp.exp(gc[u]) - iuw[u][:, dk:]).astype(BF16)
        for c in range(CHUNKS_PER_GROUP):
            for u in each:
                bn = _dot(jnp.where(lane_chunk == c, k_til_t[u], jnp.zeros_like(k_til_t[u])), uwb[u])
                idx = (run * unroll + u) * CHUNKS_PER_GROUP + c
                b_s[head, idx] = bn[:, :dk]
                n_s[head, idx] = (-bn[:, dk:]).astype(BF16)
                gam_s[head, idx] = jnp.exp(gl[u][c * GDN_CHUNK:c * GDN_CHUNK + 8, :])

    stage(0, stage_sets[0])

    def item_pair(m, carry):
        w = 2 * m
        chains(w, stage_sets[0])
        stage(w + 1, stage_sets[1])
        chains(w + 1, stage_sets[1])
        stage(jnp.minimum(w + 2, n_items - 1), stage_sets[0])
        return carry

    lax.fori_loop(0, n_items // 2, item_pair, 0)

    def scan(c, states):
        rows = pl.ds(pl.multiple_of(c * GDN_CHUNK, GDN_CHUNK), GDN_CHUNK)
        finalize(jnp.maximum(c - 1, 0))
        new_states = []
        for head in range(GDN_HEADS_PER_STEP):
            sb = states[head].astype(BF16)
            o_s[head, rows, :] = o_s[head, rows, :] + _dot(qh_s[head, rows, :], sb)
            gam = gam_s[head, c][0:1, :]
            new_states.append(states[head] * gam + (_dot(n_s[head, c], sb) + b_s[head, c]))
        return tuple(new_states)

    def finalize(c):
        rows = pl.ds(pl.multiple_of(c * GDN_CHUNK, GDN_CHUNK), GDN_CHUNK)
        for head in range(GDN_HEADS_PER_STEP):
            o = o_s[head, rows, :]
            ms = jnp.mean(o * o, axis=-1, keepdims=True)
            o = o * lax.rsqrt(ms + EPS) * nw_ref[...]
            gate = gate_ref[head, 0, rows, :].astype(F32)
            o_ref[0, rows, head * LANES:(head + 1) * LANES] = (o * _silu(gate)).astype(BF16)

    state0 = jnp.zeros((dk, dk), F32)
    lax.fori_loop(0, n_chunks, scan, (state0,) * GDN_HEADS_PER_STEP)
    finalize(n_chunks - 1)


def _gdn(side, ab, conv_w, alog_b, dtb_b, norm_w, batch, seq):
    side4 = side.reshape(SIDE_BLOCKS, batch, seq, LANES)
    ab3 = ab.reshape(batch, seq, AB_WIDTH)
    n_chunks = seq // GDN_CHUNK
    dk = GDN_HEAD_DIM
    hps = GDN_HEADS_PER_STEP
    steps = GDN_HEADS // hps
    run_rows = min(GDN_GROUP_UNROLL, seq // GDN_GROUP) * GDN_GROUP
    assert (hps * (seq // run_rows)) % 2 == 0

    def head_slabs(base):
        return pl.BlockSpec((hps, 1, seq, LANES), lambda b, h: (base // hps + h, b, 0, 0))

    head_vec = pl.BlockSpec((hps, 1, LANES), lambda b, h: (h, 0, 0))
    staging = [pltpu.VMEM((run_rows, LANES), F32)] * 12
    return pl.pallas_call(
        _gdn_kernel,
        out_shape=jax.ShapeDtypeStruct((batch, seq, GDN_WIDTH), BF16),
        grid=(batch, steps),
        in_specs=[
            head_slabs(SIDE_GDN_Q), head_slabs(SIDE_GDN_K), head_slabs(SIDE_GDN_V),
            head_slabs(SIDE_GDN_GATE),
            pl.BlockSpec((1, seq, AB_WIDTH), lambda b, h: (b, 0, 0)),
            pl.BlockSpec((3, hps, CONV_K, LANES), lambda b, h: (0, h, 0, 0)),
            head_vec, head_vec,
            pl.BlockSpec((1, LANES), lambda b, h: (0, 0)),
        ],
        out_specs=pl.BlockSpec((1, seq, hps * LANES), lambda b, h: (b, 0, h)),
        scratch_shapes=staging + [
            pltpu.VMEM((3, 8 + GDN_GROUP, LANES), F32),
            pltpu.VMEM((hps, n_chunks, dk, dk), BF16),
            pltpu.VMEM((hps, n_chunks, dk, dk), F32),
            pltpu.VMEM((hps, seq, dk), BF16),
            pltpu.VMEM((hps, n_chunks, 8, LANES), F32),
            pltpu.VMEM((hps, seq, dk), F32),
        ],
        compiler_params=pltpu.CompilerParams(
            dimension_semantics=("arbitrary", "arbitrary"), vmem_limit_bytes=VMEM_LIMIT),
        name="gdn",
    )(side4, side4, side4, side4, ab3, conv_w, alog_b, dtb_b, norm_w)


def _out_proj_kernel(x_ref, osb_ref, og_ref, msb_ref, mg_ref, wsb_ref, wg_ref, wo_ref, pw_ref, o_ref):
    ya = _dot(osb_ref[...], wsb_ref[...])
    yb = _dot(og_ref[...], wg_ref[...])
    merged = []
    for k in range(D_MODEL // LANES):
        sl = slice(k * LANES, (k + 1) * LANES)
        merged.append((_sigmoid(msb_ref[k].astype(F32)) * ya[:, sl]
                       + _sigmoid(mg_ref[k].astype(F32)) * yb[:, sl]).astype(BF16))
    y = _dot(jnp.concatenate(merged, axis=1), wo_ref[...])
    ms = jnp.mean(y * y, axis=-1, keepdims=True)
    o_ref[...] = x_ref[...] + y * lax.rsqrt(ms + EPS) * pw_ref[...]


def _out_proj(x2d, o_sb, o_g, side, w_sb, w_g, w_o, post_w):
    n = x2d.shape[0]
    tm = ROW_TILE
    const = lambda i: (0, 0)
    merge_blocks = D_MODEL // LANES
    return pl.pallas_call(
        _out_proj_kernel,
        out_shape=jax.ShapeDtypeStruct((n, D_MODEL), F32),
        grid=(n // tm,),
        in_specs=[
            pl.BlockSpec((tm, D_MODEL), lambda i: (i, 0)),
            pl.BlockSpec((tm, SB_WIDTH), lambda i: (i, 0)),
            pl.BlockSpec((tm, GDN_WIDTH), lambda i: (i, 0)),
            pl.BlockSpec((merge_blocks, tm, LANES), lambda i: (SIDE_MERGE_SB // merge_blocks, i, 0)),
            pl.BlockSpec((merge_blocks, tm, LANES), lambda i: (SIDE_MERGE_GDN // merge_blocks, i, 0)),
            pl.BlockSpec((SB_WIDTH, D_MODEL), const),
            pl.BlockSpec((GDN_WIDTH, D_MODEL), const),
            pl.BlockSpec((D_MODEL, D_MODEL), const),
            pl.BlockSpec((1, D_MODEL), const),
        ],
        out_specs=pl.BlockSpec((tm, D_MODEL), lambda i: (i, 0)),
        compiler_params=pltpu.CompilerParams(
            dimension_semantics=("arbitrary",), vmem_limit_bytes=VMEM_LIMIT),
        name="out_proj",
    )(x2d, o_sb, o_g, side, side, w_sb, w_g, w_o, post_w)


W_OFF_SB_GATE = QKV_WIDTH
W_OFF_AB = W_OFF_SB_GATE + SB_WIDTH + 3 * GDN_WIDTH + GDN_WIDTH
W_OFF_MERGE = W_OFF_AB + 2 * GDN_HEADS
W_IN_WIDTH = W_OFF_MERGE + 2 * D_MODEL
W_PREP_ROWS = 128


def _w_prep_kernel(wt_ref, wa_ref, wb_ref, wab_ref):
    def cols(lo, hi):
        return wt_ref[0, lo:hi, :].T

    wa_ref[0, :, :SB_WIDTH] = (cols(0, SB_WIDTH) * (SB_HEAD_DIM ** -0.5 * LOG2_E)).astype(BF16)
    wa_ref[0, :, SB_WIDTH:] = cols(SB_WIDTH, QKV_WIDTH).astype(BF16)
    wb_ref[0, :, :2 * D_MODEL] = cols(W_OFF_MERGE, W_IN_WIDTH).astype(BF16)
    wb_ref[0, :, 2 * D_MODEL:] = cols(W_OFF_SB_GATE, W_OFF_AB).astype(BF16)
    ab = wt_ref[0, W_OFF_AB:W_OFF_MERGE, :]
    pad = jnp.zeros((AB_WIDTH - 2 * GDN_HEADS, ab.shape[1]), F32)
    wab_ref[0] = jnp.concatenate([ab, pad], axis=0).T.astype(BF16)


def _regroup_w_in(w):
    depth = w.shape[0]
    assert w.shape[1:] == (D_MODEL, W_IN_WIDTH)
    rows = W_PREP_ROWS

    def out(width):
        return pl.BlockSpec((1, rows, width), lambda l, r: (l, r, 0))

    return pl.pallas_call(
        _w_prep_kernel,
        out_shape=(jax.ShapeDtypeStruct((depth, D_MODEL, QKV_WIDTH), BF16),
                   jax.ShapeDtypeStruct((depth, D_MODEL, SIDE_WIDTH), BF16),
                   jax.ShapeDtypeStruct((depth, D_MODEL, AB_WIDTH), BF16)),
        grid=(depth, D_MODEL // rows),
        in_specs=[pl.BlockSpec((1, W_IN_WIDTH, rows), lambda l, r: (l, 0, r))],
        out_specs=(out(QKV_WIDTH), out(SIDE_WIDTH), out(AB_WIDTH)),
        compiler_params=pltpu.CompilerParams(
            dimension_semantics=("arbitrary", "arbitrary"), vmem_limit_bytes=VMEM_LIMIT),
        name="w_prep",
    )(jnp.swapaxes(w, 1, 2))


def kernel(x, pre_norm_w, post_norm_w, w_in, conv_w, a_log, dt_bias, gdn_norm_w,
           w_branch_sb, w_branch_gdn, w_out):
    batch, seq, d = x.shape
    depth = w_in.shape[0]
    assert d == D_MODEL and seq % GDN_GROUP == 0 and seq % (2 * SB_TILE) == 0 and (batch * seq) % ROW_TILE == 0
    wa, wb, wab = _regroup_w_in(w_in)
    cw = conv_w.astype(F32).reshape(depth, CONV_K, 3, GDN_HEADS, LANES).transpose(0, 2, 3, 1, 4)
    alog_b = jnp.broadcast_to(a_log.astype(F32)[:, :, None, None], (depth, GDN_HEADS, 1, LANES))
    dtb_b = jnp.broadcast_to(dt_bias.astype(F32)[:, :, None, None], (depth, GDN_HEADS, 1, LANES))
    w_sb = w_branch_sb.astype(BF16)
    w_g = w_branch_gdn.astype(BF16)
    w_o = w_out.astype(BF16)
    x2d = x.reshape(batch * seq, d)
    for layer in range(depth):
        qkv, side, ab = _in_proj(x2d, pre_norm_w[layer][None, :], wa[layer], wb[layer], wab[layer])
        o_sb = _sb_attention(qkv, side, batch, seq)
        o_g = _gdn(side, ab, cw[layer], alog_b[layer], dtb_b[layer], gdn_norm_w[layer][None, :], batch, seq)
        x2d = _out_proj(x2d, o_sb.reshape(batch * seq, SB_WIDTH), o_g.reshape(batch * seq, GDN_WIDTH),
                        side, w_sb[layer], w_g[layer], w_o[layer], post_norm_w[layer][None, :])
    return x2d.reshape(batch, seq, d)
```
